```python
import math
import jax, jax.numpy as jnp
from jax import lax
import numpy as np

D_MODEL = 2048
BATCH = 2
SEQ = 8192
DEPTH = 2

QBLK = 128
ROPE_THETA = 500000.0
EPS = 1e-6
TINY = 1e-30
BIG = 1e9

SB_HEADS = 8
SB_DIM = 128
MLA_HEADS = 8
MLA_Q_RANK = 512
MLA_KV_RANK = 256
MLA_NOPE = 128
MLA_ROPE = 64
MLA_V = 128
NSA_HEADS = 16
NSA_GROUPS = 2
NSA_DIM = 128
ROT_DIM = NSA_DIM // 4
CMP_BLK = 32
CMP_STRIDE = 16
CMP_HIDDEN = 256
SLC_BLK = 64
SLC_TOPK = 16
WINDOW = 512
PEER_HEADS = 8
PEER_KEYS = 128
PEER_DKEY = 256
PEER_TOPK = 16
N_EXPERTS = PEER_KEYS * PEER_KEYS
PEER_CHUNK = 64

N_EVEN = (DEPTH + 1) // 2
N_ODD = DEPTH // 2
SBMLA_IN = 3 * SB_HEADS * SB_DIM + MLA_Q_RANK + MLA_KV_RANK + MLA_ROPE
SBMLA_OUT = SB_HEADS * SB_DIM + MLA_HEADS * MLA_V
NSA_IN = NSA_HEADS * NSA_DIM + 6 * NSA_GROUPS * NSA_DIM + 3 * NSA_HEADS

kernel_name = "hybrid_sb_mla_nsa_peer_adaln"


def rmsnorm(x, g):
    xf = x.astype(jnp.float32)
    y = xf * lax.rsqrt(jnp.mean(xf * xf, axis=-1, keepdims=True) + EPS)
    return (y * g.astype(jnp.float32)).astype(x.dtype)


def rope(x, pos, rot_dim):
    half = rot_dim // 2
    inv = ROPE_THETA ** (-jnp.arange(half, dtype=jnp.float32) / half)
    ang = pos.astype(jnp.float32)[..., None] * inv
    cos = jnp.cos(ang)[:, :, None, :]
    sin = jnp.sin(ang)[:, :, None, :]
    x1 = x[..., :half].astype(jnp.float32)
    x2 = x[..., half:rot_dim].astype(jnp.float32)
    r1 = (x1 * cos - x2 * sin).astype(x.dtype)
    r2 = (x2 * cos + x1 * sin).astype(x.dtype)
    return jnp.concatenate([r1, r2, x[..., rot_dim:]], axis=-1)


def masked_softmax(z, mask):
    z = jnp.where(mask, z, -jnp.inf)
    m = jnp.max(z, axis=-1, keepdims=True)
    m = jnp.where(jnp.isfinite(m), m, 0.0)
    e = jnp.exp(z - m)
    return e / jnp.maximum(jnp.sum(e, axis=-1, keepdims=True), TINY)


def modulation(c, w, b):
    m = jax.nn.silu(c) @ w + b
    shift, scale, gate = jnp.split(m, 3, axis=-1)
    return shift[:, None, :], scale[:, None, :], gate[:, None, :]


def sweep(fn, block, *arrs):
    n = arrs[0].shape[1] // block
    def body(s0):
        return fn(s0, *[lax.dynamic_slice_in_dim(a, s0, block, axis=1) for a in arrs])
    out = lax.map(body, jnp.arange(n, dtype=jnp.int32) * block)
    out = jnp.moveaxis(out, 0, 1)
    return out.reshape((out.shape[0], n * block) + out.shape[3:])


def stick_breaking(q, k, v):
    S = q.shape[1]
    scale = 1.0 / math.sqrt(SB_DIM)
    kpos = jnp.arange(S)
    def blk(q0, qb):
        z = jnp.einsum('bqhd,bkhd->bhqk', qb, k).astype(jnp.float32) * scale
        t = q0 + jnp.arange(QBLK)
        strict = kpos[None, :] < t[:, None]
        log_1m = jnp.where(strict, jax.nn.log_sigmoid(-z), 0.0)
        cs = jnp.cumsum(log_1m, axis=-1)
        suffix = cs[..., -1:] - cs
        a = jnp.where(strict, jnp.exp(jax.nn.log_sigmoid(z) + suffix), 0.0)
        return jnp.einsum('bhqk,bkhd->bqhd', a.astype(v.dtype), v)
    return sweep(blk, QBLK, q)


def mla(c_q, c_kv, k_r, pos, q_norm, w_uq, kv_norm, w_ukv):
    B, S, _ = c_q.shape
    q = (rmsnorm(c_q, q_norm) @ w_uq).reshape(B, S, MLA_HEADS, MLA_NOPE + MLA_ROPE)
    q_nope = q[..., :MLA_NOPE]
    q_rope = rope(q[..., MLA_NOPE:], pos, MLA_ROPE)
    kv = (rmsnorm(c_kv, kv_norm) @ w_ukv).reshape(B, S, MLA_HEADS, MLA_NOPE + MLA_V)
    k_nope, v = kv[..., :MLA_NOPE], kv[..., MLA_NOPE:]
    k_rope = rope(k_r[:, :, None, :], pos, MLA_ROPE)[:, :, 0, :]
    scale = 1.0 / math.sqrt(MLA_NOPE + MLA_ROPE)
    kpos = jnp.arange(S)
    def blk(q0, qn, qr):
        t = q0 + jnp.arange(QBLK)
        z = (jnp.einsum('bqhd,bkhd->bhqk', qn, k_nope)
             + jnp.einsum('bqhr,bkr->bhqk', qr, k_rope)).astype(jnp.float32) * scale
        p = masked_softmax(z, kpos[None, :] <= t[:, None])
        return jnp.einsum('bhqk,bkhd->bqhd', p.astype(v.dtype), v)
    return sweep(blk, QBLK, q_nope, q_rope)


def sb_mla_mixer(h, pos, w_in, q_norm, w_uq, kv_norm, w_ukv, w_out):
    B, S, _ = h.shape
    sbw = SB_HEADS * SB_DIM
    splits = np.cumsum([sbw, sbw, sbw, MLA_Q_RANK, MLA_KV_RANK]).tolist()
    q_sb, k_sb, v_sb, c_q, c_kv, k_r = jnp.split(h @ w_in, splits, axis=-1)
    shp = (B, S, SB_HEADS, SB_DIM)
    o_a = stick_breaking(q_sb.reshape(shp), k_sb.reshape(shp), v_sb.reshape(shp))
    o_b = mla(c_q, c_kv, k_r, pos, q_norm, w_uq, kv_norm, w_ukv)
    o = jnp.concatenate([o_a.reshape(B, S, -1), o_b.reshape(B, S, -1)], axis=-1)
    return o @ w_out


def nsa_mixer(h, pos, w_in, pe_k, pe_v, w1_k, w2_k, w1_v, w2_v, w_out):
    B, S, _ = h.shape
    G, R, d = NSA_GROUPS, NSA_HEADS // NSA_GROUPS, NSA_DIM
    kvw = G * d
    splits = np.cumsum([NSA_HEADS * d, kvw, kvw, kvw, kvw, kvw, kvw]).tolist()
    q, kc, vc, ks, vs, kw, vw, gl = jnp.split(h @ w_in, splits, axis=-1)
    q = rope(q.reshape(B, S, NSA_HEADS, d), pos, ROT_DIM).reshape(B, S, G, R, d)
    kc = rope(kc.reshape(B, S, G, d), pos, ROT_DIM)
    ks = rope(ks.reshape(B, S, G, d), pos, ROT_DIM)
    kw = rope(kw.reshape(B, S, G, d), pos, ROT_DIM)
    vc, vs, vw = (a.reshape(B, S, G, d) for a in (vc, vs, vw))
    gates = jax.nn.sigmoid(gl.reshape(B, S, G, R, 3))

    n_cmp = (S - CMP_BLK) // CMP_STRIDE + 1
    cidx = np.arange(n_cmp)[:, None] * CMP_STRIDE + np.arange(CMP_BLK)[None, :]
    def compress(t, pe, w1, w2):
        blk = t[:, cidx] + pe[None, None, :, None, :]
        blk = blk.transpose(0, 1, 3, 2, 4).reshape(B, n_cmp, G, CMP_BLK * d)
        return jax.nn.gelu(blk @ w1) @ w2
    kc_c = compress(kc, pe_k, w1_k, w2_k)
    vc_c = compress(vc, pe_v, w1_v, w2_v)
    cmp_end = jnp.asarray(cidx[:, -1])

    n_slc = S // SLC_BLK
    slc_k = min(SLC_TOPK, n_slc)
    c_s = np.arange(n_cmp) * CMP_STRIDE
    s_s = np.arange(n_slc) * SLC_BLK
    overlap = np.clip(np.minimum(c_s[:, None] + CMP_BLK, s_s[None, :] + SLC_BLK)
                      - np.maximum(c_s[:, None], s_s[None, :]), 0, None)
    overlap = jnp.asarray(overlap.astype(np.float32))
    ks_b = ks.reshape(B, n_slc, SLC_BLK, G, d).transpose(0, 3, 1, 2, 4)
    vs_b = vs.reshape(B, n_slc, SLC_BLK, G, d).transpose(0, 3, 1, 2, 4)
    blk_id = jnp.arange(n_slc)
    bi = jnp.arange(B)[:, None, None, None]
    gi = jnp.arange(G)[None, :, None, None]

    kw_p = jnp.pad(kw, ((0, 0), (WINDOW, 0), (0, 0), (0, 0)))
    vw_p = jnp.pad(vw, ((0, 0), (WINDOW, 0), (0, 0), (0, 0)))
    scale = 1.0 / math.sqrt(d)

    def blk(q0, qb, gb):
        t = q0 + jnp.arange(QBLK)
        zc = jnp.einsum('bqgrd,bngd->bgrqn', qb, kc_c).astype(jnp.float32) * scale
        pc = masked_softmax(zc, cmp_end[None, :] <= t[:, None])
        oc = jnp.einsum('bgrqn,bngd->bqgrd', pc.astype(vc_c.dtype), vc_c)
        imp = jnp.einsum('bgrqn,nj->bgqj', pc.astype(jnp.float32), overlap)
        cur = t // SLC_BLK
        forced = (blk_id[None, :] == 0) | (blk_id[None, :] == cur[:, None]) | (blk_id[None, :] == cur[:, None] - 1)
        valid = blk_id[None, :] * SLC_BLK <= t[:, None]
        score = jnp.where(forced, BIG, jnp.where(valid, imp, -BIG))
        _, sel = lax.top_k(score, slc_k)
        kg = ks_b[bi, gi, sel]
        vg = vs_b[bi, gi, sel]
        zs = jnp.einsum('bqgrd,bgqkld->bgrqkl', qb, kg).astype(jnp.float32) * scale
        kpos = sel[..., None] * SLC_BLK + jnp.arange(SLC_BLK)
        smask = (kpos <= t[:, None, None])[:, :, None].reshape(B, G, 1, QBLK, slc_k * SLC_BLK)
        ps = masked_softmax(zs.reshape(B, G, R, QBLK, slc_k * SLC_BLK), smask)
        ps = ps.reshape(B, G, R, QBLK, slc_k, SLC_BLK)
        osl = jnp.einsum('bgrqkl,bgqkld->bqgrd', ps.astype(vg.dtype), vg)
        kwb = lax.dynamic_slice_in_dim(kw_p, q0, WINDOW + QBLK, axis=1)
        vwb = lax.dynamic_slice_in_dim(vw_p, q0, WINDOW + QBLK, axis=1)
        wpos = q0 - WINDOW + jnp.arange(WINDOW + QBLK)
        wmask = (wpos[None, :] <= t[:, None]) & (wpos[None, :] > t[:, None] - WINDOW) & (wpos[None, :] >= 0)
        zw = jnp.einsum('bqgrd,bkgd->bgrqk', qb, kwb).astype(jnp.float32) * scale
        pw = masked_softmax(zw, wmask)
        ow = jnp.einsum('bgrqk,bkgd->bqgrd', pw.astype(vwb.dtype), vwb)
        return gb[..., 0:1] * oc + gb[..., 1:2] * osl + gb[..., 2:3] * ow

    o = sweep(blk, QBLK, q, gates)
    return o.reshape(B, S, NSA_HEADS * d) @ w_out


def peer(h, w_q, k1, k2, u, v):
    B, S, D = h.shape
    half = PEER_DKEY // 2
    q = (h @ w_q).reshape(B, S, PEER_HEADS, 2, half).astype(jnp.float32)
    s1 = jnp.einsum('bshd,nd->bshn', q[..., 0, :], k1.astype(jnp.float32))
    s2 = jnp.einsum('bshd,nd->bshn', q[..., 1, :], k2.astype(jnp.float32))
    v1, i1 = lax.top_k(s1, PEER_TOPK)
    v2, i2 = lax.top_k(s2, PEER_TOPK)
    cand = (v1[..., :, None] + v2[..., None, :]).reshape(B, S, PEER_HEADS, PEER_TOPK * PEER_TOPK)
    cidx = (i1[..., :, None] * PEER_KEYS + i2[..., None, :]).reshape(B, S, PEER_HEADS, PEER_TOPK * PEER_TOPK)
    top, pos = lax.top_k(cand, PEER_TOPK)
    experts = jnp.take_along_axis(cidx, pos, axis=-1)
    g = jax.nn.softmax(top, axis=-1).astype(h.dtype)
    def chunk(s0, hb, eb, gb):
        act = jax.nn.gelu(jnp.einsum('bqd,bqhkd->bqhk', hb, u[eb]))
        return jnp.einsum('bqhk,bqhkd->bqd', gb * act, v[eb])
    return sweep(chunk, PEER_CHUNK, h, experts, g)


def setup_inputs(seed: int = 0) -> dict:
    key = jax.random.key(seed)
    ks = iter(jax.random.split(key, 40))
    def nrm(shape, std):
        return jax.random.normal(next(ks), shape, jnp.float32) * std
    def gain(shape):
        return 1.0 + nrm(shape, 0.05)
    D = D_MODEL
    x = nrm((BATCH, SEQ, D), 1.0)
    c = nrm((BATCH, D), 1.0)
    offset = jax.random.randint(next(ks), (BATCH, 1), 0, 1024, dtype=jnp.int32)
    positions = offset + jnp.arange(SEQ, dtype=jnp.int32)[None, :]
    return {
        "x": x,
        "c": c,
        "positions": positions,
        "norm_mix": gain((DEPTH, D)),
        "ada_mix_w": nrm((DEPTH, D, 3 * D), 0.5 * D ** -0.5),
        "ada_mix_b": nrm((DEPTH, 3 * D), 0.02),
        "sbmla_w_in": nrm((N_EVEN, D, SBMLA_IN), D ** -0.5),
        "mla_q_norm": gain((N_EVEN, MLA_Q_RANK)),
        "mla_w_uq": nrm((N_EVEN, MLA_Q_RANK, MLA_HEADS * (MLA_NOPE + MLA_ROPE)), MLA_Q_RANK ** -0.5),
        "mla_kv_norm": gain((N_EVEN, MLA_KV_RANK)),
        "mla_w_ukv": nrm((N_EVEN, MLA_KV_RANK, MLA_HEADS * (MLA_NOPE + MLA_V)), MLA_KV_RANK ** -0.5),
        "sbmla_w_out": nrm((N_EVEN, SBMLA_OUT, D), SBMLA_OUT ** -0.5),
        "nsa_w_in": nrm((N_ODD, D, NSA_IN), D ** -0.5),
        "nsa_pe_k": nrm((N_ODD, CMP_BLK, NSA_DIM), 0.1),
        "nsa_pe_v": nrm((N_ODD, CMP_BLK, NSA_DIM), 0.1),
        "nsa_w1_k": nrm((N_ODD, CMP_BLK * NSA_DIM, CMP_HIDDEN), (CMP_BLK * NSA_DIM) ** -0.5),
        "nsa_w2_k": nrm((N_ODD, CMP_HIDDEN, NSA_DIM), CMP_HIDDEN ** -0.5),
        "nsa_w1_v": nrm((N_ODD, CMP_BLK * NSA_DIM, CMP_HIDDEN), (CMP_BLK * NSA_DIM) ** -0.5),
        "nsa_w2_v": nrm((N_ODD, CMP_HIDDEN, NSA_DIM), CMP_HIDDEN ** -0.5),
        "nsa_w_out": nrm((N_ODD, NSA_HEADS * NSA_DIM, D), (NSA_HEADS * NSA_DIM) ** -0.5),
        "norm_ffn": gain((DEPTH, D)),
        "ada_ffn_w": nrm((DEPTH, D, 3 * D), 0.5 * D ** -0.5),
        "ada_ffn_b": nrm((DEPTH, 3 * D), 0.02),
        "peer_w_q": nrm((DEPTH, D, PEER_HEADS * PEER_DKEY), D ** -0.5),
        "peer_k1": nrm((DEPTH, PEER_KEYS, PEER_DKEY // 2), (PEER_DKEY // 2) ** -0.5),
        "peer_k2": nrm((DEPTH, PEER_KEYS, PEER_DKEY // 2), (PEER_DKEY // 2) ** -0.5),
        "peer_u": nrm((DEPTH, N_EXPERTS, D), D ** -0.5),
        "peer_v": nrm((DEPTH, N_EXPERTS, D), PEER_HEADS ** -0.5),
        "final_norm": gain((D,)),
    }


def reference(x, c, positions, norm_mix, ada_mix_w, ada_mix_b, sbmla_w_in, mla_q_norm, mla_w_uq,
              mla_kv_norm, mla_w_ukv, sbmla_w_out, nsa_w_in, nsa_pe_k, nsa_pe_v, nsa_w1_k, nsa_w2_k,
              nsa_w1_v, nsa_w2_v, nsa_w_out, norm_ffn, ada_ffn_w, ada_ffn_b, peer_w_q, peer_k1, peer_k2,
              peer_u, peer_v, final_norm):
    for layer in range(DEPTH):
        i = layer // 2
        shift, scale, gate = modulation(c, ada_mix_w[layer], ada_mix_b[layer])
        h = rmsnorm(x, norm_mix[layer]) * (1.0 + scale) + shift
        if layer % 2 == 0:
            y = sb_mla_mixer(h, positions, sbmla_w_in[i], mla_q_norm[i], mla_w_uq[i],
                             mla_kv_norm[i], mla_w_ukv[i], sbmla_w_out[i])
        else:
            y = nsa_mixer(h, positions, nsa_w_in[i], nsa_pe_k[i], nsa_pe_v[i], nsa_w1_k[i],
                          nsa_w2_k[i], nsa_w1_v[i], nsa_w2_v[i], nsa_w_out[i])
        x = x + gate * y
        shift, scale, gate = modulation(c, ada_ffn_w[layer], ada_ffn_b[layer])
        h = rmsnorm(x, norm_ffn[layer]) * (1.0 + scale) + shift
        x = x + gate * peer(h, peer_w_q[layer], peer_k1[layer], peer_k2[layer], peer_u[layer], peer_v[layer])
    return rmsnorm(x, final_norm)
```

```python
import functools
import math

import numpy as np
import jax
import jax.numpy as jnp
from jax import lax
from jax.experimental import pallas as pl
from jax.experimental.pallas import tpu as pltpu

F32 = jnp.float32
BF16 = jnp.bfloat16

QBLK = 128
ROPE_THETA = 500000.0
EPS = 1e-6
TINY = 1e-30
BIG = 1e9
NEG = -1e30

SB_HEADS = 8
SB_DIM = 128
MLA_HEADS = 8
MLA_Q_RANK = 512
MLA_KV_RANK = 256
MLA_NOPE = 128
MLA_ROPE = 64
MLA_V = 128
NSA_HEADS = 16
NSA_GROUPS = 2
NSA_DIM = 128
ROT_DIM = NSA_DIM // 4
CMP_BLK = 32
CMP_STRIDE = 16
CMP_HIDDEN = 256
SLC_BLK = 64
SLC_TOPK = 16
WINDOW = 512
PEER_HEADS = 8
PEER_KEYS = 128
PEER_DKEY = 256
PEER_TOPK = 16

LANES = 128
VMEM_LIMIT = 56 * 1024 * 1024


def _cparams(sem):
    return pltpu.CompilerParams(dimension_semantics=sem, vmem_limit_bytes=VMEM_LIMIT)


def _split(a):
    hi = a.astype(BF16)
    lo = (a - hi.astype(F32)).astype(BF16)
    return hi, lo


def _dot(a, b):
    return jnp.dot(a, b, preferred_element_type=F32)


def _dot_nt(a, b):
    return lax.dot_general(a, b, (((1,), (1,)), ((), ())), preferred_element_type=F32)


def _dot3(a, b):
    ah, al = _split(a)
    bh, bl = _split(b)
    return _dot(ah, bh) + (_dot(ah, bl) + _dot(al, bh))


def _dot3_nt(a, b):
    ah, al = _split(a)
    bh, bl = _split(b)
    return _dot_nt(ah, bh) + (_dot_nt(ah, bl) + _dot_nt(al, bh))


def _gelu_tanh(x):
    return 0.5 * x * (1.0 + jnp.tanh(math.sqrt(2.0 / math.pi) * (x + 0.044715 * (x * x * x))))


def _pad_cols(w, n):
    return jnp.pad(w, ((0, 0), (0, n - w.shape[1])))


def _mod_kernel(c_ref, w_ref, b_ref, o_ref):
    c = c_ref[...]
    s = c * jax.nn.sigmoid(c)
    o_ref[0] = _dot3(s, w_ref[0]) + b_ref[0]


def modulation_all(c, w, b):
    L, D, N = w.shape
    B = c.shape[0]
    rows = 8
    cp = jnp.pad(c, ((0, rows - B), (0, 0)))
    tn = 768 if N % 768 == 0 else N
    out = pl.pallas_call(
        _mod_kernel,
        grid=(L, N // tn),
        in_specs=[
            pl.BlockSpec((rows, D), lambda l, j: (0, 0)),
            pl.BlockSpec((1, D, tn), lambda l, j: (l, 0, j)),
            pl.BlockSpec((1, 1, tn), lambda l, j: (l, 0, j)),
        ],
        out_specs=pl.BlockSpec((1, rows, tn), lambda l, j: (l, 0, j)),
        out_shape=jax.ShapeDtypeStruct((L, rows, N), F32),
        compiler_params=_cparams(("arbitrary", "arbitrary")),
        name="adaln_mod",
    )(cp, w, b.reshape(L, 1, N))
    return out[:, :B]


def _norm_mm_kernel(x_ref, g_ref, sc_ref, sh_ref, w_ref, o_ref, *rest, emit_h):
    if emit_h:
        h_ref, hb_ref = rest
    else:
        (hb_ref,) = rest

    @pl.when(pl.program_id(1) == 0)
    def _():
        x = x_ref[...]
        y = x * lax.rsqrt(jnp.mean(x * x, axis=-1, keepdims=True) + EPS)
        h = (y * g_ref[...]) * (1.0 + sc_ref[...]) + sh_ref[...]
        hb_ref[...] = h.astype(BF16)
        if emit_h:
            h_ref[...] = h

    o_ref[...] = _dot(hb_ref[...], w_ref[...]).astype(o_ref.dtype)


def norm_matmul(x, g, scale, shift, w, *, seq, tm, tn, emit_h=False, out_dtype=F32):
    T, K = x.shape
    N = w.shape[1]
    assert T % tm == 0 and N % tn == 0 and seq % tm == 0
    nb = seq // tm
    B = scale.shape[0]
    out_shape = [jax.ShapeDtypeStruct((T, N), out_dtype)]
    out_specs = [pl.BlockSpec((tm, tn), lambda i, j: (i, j))]
    if emit_h:
        out_shape.append(jax.ShapeDtypeStruct((T, K), F32))
        out_specs.append(pl.BlockSpec((tm, K), lambda i, j: (i, 0)))
    res = pl.pallas_call(
        functools.partial(_norm_mm_kernel, emit_h=emit_h),
        grid=(T // tm, N // tn),
        in_specs=[
            pl.BlockSpec((tm, K), lambda i, j: (i, 0)),
            pl.BlockSpec((1, K), lambda i, j: (0, 0)),
            pl.BlockSpec((None, 1, K), lambda i, j: (i // nb, 0, 0)),
            pl.BlockSpec((None, 1, K), lambda i, j: (i // nb, 0, 0)),
            pl.BlockSpec((K, tn), lambda i, j: (0, j)),
        ],
        out_specs=out_specs,
        out_shape=out_shape,
        scratch_shapes=[pltpu.VMEM((tm, K), BF16)],
        compiler_params=_cparams(("arbitrary", "arbitrary")),
        name="norm_matmul",
    )(x, g.reshape(1, K), scale.reshape(B, 1, K), shift.reshape(B, 1, K), w)
    return res if emit_h else res[0]


def _mm_res_kernel(*refs, n_in):
    a_refs = refs[:n_in]
    w_refs = refs[n_in:2 * n_in]
    x_ref, gate_ref, o_ref = refs[2 * n_in:]
    y = _dot(a_refs[0][...], w_refs[0][...])
    for a_ref, w_ref in zip(a_refs[1:], w_refs[1:]):
        y = y + _dot(a_ref[...], w_ref[...])
    o_ref[...] = x_ref[...] + gate_ref[...] * y


def matmul_residual(a_list, w_list, x, gate, *, seq, tm, tn):
    T, D = x.shape
    B = gate.shape[0]
    nb = seq // tm
    n_in = len(a_list)
    in_specs = [pl.BlockSpec((tm, a.shape[1]), lambda i, j: (i, 0)) for a in a_list]
    in_specs += [pl.BlockSpec((w.shape[0], tn), lambda i, j: (0, j)) for w in w_list]
    in_specs += [
        pl.BlockSpec((tm, tn), lambda i, j: (i, j)),
        pl.BlockSpec((None, 1, tn), lambda i, j: (i // nb, 0, j)),
    ]
    return pl.pallas_call(
        functools.partial(_mm_res_kernel, n_in=n_in),
        grid=(T // tm, D // tn),
        in_specs=in_specs,
        out_specs=pl.BlockSpec((tm, tn), lambda i, j: (i, j)),
        out_shape=jax.ShapeDtypeStruct((T, D), F32),
        compiler_params=_cparams(("arbitrary", "arbitrary")),
        name="matmul_residual",
    )(*a_list, *w_list, x, gate.reshape(B, 1, D))


def _sb_kernel(q_ref, k_ref, v_ref, o_ref, *, tq, scale):
    i = pl.program_id(1)
    tk = tq
    q = q_ref[...].astype(BF16)
    t_idx = i * tq + lax.broadcasted_iota(jnp.int32, (tq, 1), 0)
    col = lax.broadcasted_iota(jnp.int32, (1, tk), 1)
    tri = (lax.broadcasted_iota(jnp.int32, (tk, tk), 0) > lax.broadcasted_iota(jnp.int32, (tk, tk), 1)).astype(BF16)

    def body(n, carry):
        c, acc = carry
        j = i - n
        start = pl.multiple_of(j * tk, tk)
        k = k_ref[pl.ds(start, tk), :].astype(BF16)
        v = v_ref[pl.ds(start, tk), :].astype(BF16)
        z = _dot_nt(q, k) * scale
        sp = jnp.maximum(z, 0.0) + jnp.log(1.0 + jnp.exp(-jnp.abs(z)))
        strict = (start + col) < t_idx
        l1m = jnp.where(strict, -sp, 0.0)
        hi, lo = _split(l1m)
        suf = _dot(hi, tri) + _dot(lo, tri)
        a = jnp.where(strict, jnp.exp((z - sp) + (suf + c)), 0.0)
        acc = acc + _dot(a.astype(BF16), v)
        c = c + jnp.sum(l1m, axis=-1, keepdims=True)
        return c, acc

    c0 = jnp.zeros((tq, 1), F32)
    acc0 = jnp.zeros((tq, v_ref.shape[1]), F32)
    _, acc = lax.fori_loop(0, i + 1, body, (c0, acc0))
    o_ref[...] = acc.astype(o_ref.dtype)


def stick_breaking_attn(p, *, batch, seq, heads, dim, q_col, k_col, v_col, tq=256):
    nq = seq // tq
    return pl.pallas_call(
        functools.partial(_sb_kernel, tq=tq, scale=1.0 / math.sqrt(dim)),
        grid=(batch * heads, nq),
        in_specs=[
            pl.BlockSpec((tq, dim), lambda bh, i: ((bh // heads) * nq + i, q_col + bh % heads)),
            pl.BlockSpec((seq, dim), lambda bh, i: (bh // heads, k_col + bh % heads)),
            pl.BlockSpec((seq, dim), lambda bh, i: (bh // heads, v_col + bh % heads)),
        ],
        out_specs=pl.BlockSpec((tq, dim), lambda bh, i: ((bh // heads) * nq + i, bh % heads)),
        out_shape=jax.ShapeDtypeStruct((batch * seq, heads * dim), BF16),
        compiler_params=_cparams(("arbitrary", "arbitrary")),
        name="stick_breaking",
    )(p, p, p)


def _flash_kernel(q_ref, k_ref, v_ref, o_ref, m_ref, l_ref, acc_ref, *, qt, tk, scale, window):
    i = pl.program_id(1)
    rb = q_ref.shape[0]
    q = q_ref[...]
    q0 = i * qt
    row = lax.broadcasted_iota(jnp.int32, (rb, 1), 0)
    tok = q0 + (row % qt if rb != qt else row)
    col = lax.broadcasted_iota(jnp.int32, (1, tk), 1)
    hi_blk = (q0 + qt - 1) // tk
    lo_blk = 0 if window is None else jnp.maximum(q0 - window + 1, 0) // tk
    m_ref[...] = jnp.full(m_ref.shape, NEG, F32)
    l_ref[...] = jnp.zeros(l_ref.shape, F32)
    acc_ref[...] = jnp.zeros(acc_ref.shape, F32)

    def body(j, carry):
        start = pl.multiple_of(j * tk, tk)
        k = k_ref[pl.ds(start, tk), :]
        v = v_ref[pl.ds(start, tk), :]
        s = _dot_nt(q, k) * scale
        kpos = start + col
        mask = kpos <= tok
        if window is not None:
            mask = mask & (kpos > tok - window)
        m_prev = m_ref[...]
        m_new = jnp.maximum(m_prev, jnp.max(jnp.where(mask, s, NEG), axis=-1, keepdims=True))
        p = jnp.where(mask, jnp.exp(s - m_new), 0.0)
        alpha = jnp.exp(m_prev - m_new)
        l_ref[...] = alpha * l_ref[...] + jnp.sum(p, axis=-1, keepdims=True)
        acc_ref[...] = alpha * acc_ref[...] + _dot(p.astype(BF16), v)
        m_ref[...] = m_new
        return carry

    lax.fori_loop(lo_blk, hi_blk + 1, body, 0)
    o_ref[...] = (acc_ref[...] / jnp.maximum(l_ref[...], TINY)).astype(o_ref.dtype)


def flash_attn(q, k, v, *, qt, tk, scale, window=None, kv_rep=1, out_dtype=BF16):
    BH, nQ, rb, dq = q.shape
    S = k.shape[1]
    dv = v.shape[2]
    return pl.pallas_call(
        functools.partial(_flash_kernel, qt=qt, tk=tk, scale=scale, window=window),
        grid=(BH, nQ),
        in_specs=[
            pl.BlockSpec((None, None, rb, dq), lambda bh, i: (bh, i, 0, 0)),
            pl.BlockSpec((None, S, dq), lambda bh, i: (bh // kv_rep, 0, 0)),
            pl.BlockSpec((None, S, dv), lambda bh, i: (bh // kv_rep, 0, 0)),
        ],
        out_specs=pl.BlockSpec((None, None, rb, dv), lambda bh, i: (bh, i, 0, 0)),
        out_shape=jax.ShapeDtypeStruct((BH, nQ, rb, dv), out_dtype),
        scratch_shapes=[pltpu.VMEM((rb, 1), F32), pltpu.VMEM((rb, 1), F32), pltpu.VMEM((rb, dv), F32)],
        compiler_params=_cparams(("arbitrary", "arbitrary")),
        name="flash_attn",
    )(q, k, v)


def _rope_tables(pos, rot_dim, period):
    half = rot_dim // 2
    inv = ROPE_THETA ** (-jnp.arange(half, dtype=F32) / half)
    ang = pos.astype(F32)[..., None] * inv
    cos, sin = jnp.cos(ang), jnp.sin(ang)
    rest = period - rot_dim
    shp = cos.shape[:-1]
    c = jnp.concatenate([cos, cos, jnp.ones(shp + (rest,), F32)], -1)
    sn = jnp.concatenate([-sin, jnp.zeros(shp + (half + rest,), F32)], -1)
    sp = jnp.concatenate([jnp.zeros(shp + (half,), F32), sin, jnp.zeros(shp + (rest,), F32)], -1)
    rep = LANES // period
    return tuple(jnp.tile(t, (1, 1, rep)) for t in (c, sn, sp))


def _rope_lanes(x, c, sn, sp, half):
    return x * c + pltpu.roll(x, LANES - half, 1) * sn + pltpu.roll(x, half, 1) * sp


def _rope_kernel(x_ref, c_ref, sn_ref, sp_ref, o_ref, *, half, n_heads, stacked):
    c, sn, sp = c_ref[...], sn_ref[...], sp_ref[...]
    for h in range(n_heads):
        y = _rope_lanes(x_ref[:, h * LANES:(h + 1) * LANES], c, sn, sp, half).astype(o_ref.dtype)
        if stacked:
            o_ref[h] = y
        else:
            o_ref[:, h * LANES:(h + 1) * LANES] = y


def rope_cols(p, tables, *, seq, col0, n_heads, half, tt, stacked_groups=None):
    T = p.shape[0]
    nb = seq // tt
    width = n_heads * LANES
    tspec = pl.BlockSpec((None, tt, LANES), lambda i, g: (i // nb, i % nb, 0))
    if stacked_groups is None:
        grid = (T // tt, 1)
        out_spec = pl.BlockSpec((tt, width), lambda i, g: (i, 0))
        out_shape = jax.ShapeDtypeStruct((T, width), BF16)
    else:
        grid = (T // tt, stacked_groups)
        out_spec = pl.BlockSpec((None, None, n_heads, tt, LANES), lambda i, g: (i, g, 0, 0, 0))
        out_shape = jax.ShapeDtypeStruct((T // tt, stacked_groups, n_heads, tt, LANES), BF16)
    return pl.pallas_call(
        functools.partial(_rope_kernel, half=half, n_heads=n_heads, stacked=stacked_groups is not None),
        grid=grid,
        in_specs=[pl.BlockSpec((tt, width), lambda i, g: (i, col0 + g)), tspec, tspec, tspec],
        out_specs=out_spec,
        out_shape=out_shape,
        compiler_params=_cparams(("arbitrary", "arbitrary")),
        name="rope",
    )(p, *tables)


def _compress_kernel(t_ref, pe_ref, w1_ref, w2_ref, o_ref):
    t = t_ref[...]
    half = t.shape[1]
    w1a = w1_ref[:half, :]
    w1b = w1_ref[half:, :]
    pe = pe_ref[...]
    bias = _dot3(jnp.broadcast_to(pe[:, :half], (8, half)), w1a.astype(F32)) + _dot3(
        jnp.broadcast_to(pe[:, half:], (8, half)), w1b.astype(F32))
    a = _dot(t, w1a)
    b = _dot(t, w1b)
    n = a.shape[0]
    hid = a + pltpu.roll(b, n - 1, 0) + bias[0:1, :]
    o_ref[...] = _dot(_gelu_tanh(hid).astype(BF16), w2_ref[...]).astype(o_ref.dtype)


def nsa_compress(t2, pe, w1, w2):
    B, G, M, K = t2.shape
    d = w2.shape[1]
    return pl.pallas_call(
        _compress_kernel,
        grid=(B, G),
        in_specs=[
            pl.BlockSpec((None, None, M, K), lambda b, g: (b, g, 0, 0)),
            pl.BlockSpec((1, 2 * K), lambda b, g: (0, 0)),
            pl.BlockSpec((2 * K, w1.shape[1]), lambda b, g: (0, 0)),
            pl.BlockSpec(w2.shape, lambda b, g: (0, 0)),
        ],
        out_specs=pl.BlockSpec((None, None, M, d), lambda b, g: (b, g, 0, 0)),
        out_shape=jax.ShapeDtypeStruct((B, G, M, d), BF16),
        compiler_params=_cparams(("arbitrary", "arbitrary")),
        name="nsa_compress",
    )(t2, pe.reshape(1, 2 * K), w1.astype(BF16), w2.astype(BF16))


def _cmp_select_kernel(q_ref, kc_ref, vc_ref, ov_ref, oc_ref, bias_ref, *, qt, n_cmp, n_slc, slc_k, scale):
    i = pl.program_id(2)
    rb = q_ref.shape[0]
    R = rb // qt
    ncp = kc_ref.shape[0]
    q = q_ref[...]
    tok = i * qt + lax.broadcasted_iota(jnp.int32, (rb, 1), 0) % qt
    n_id = lax.broadcasted_iota(jnp.int32, (1, ncp), 1)
    valid = ((n_id * CMP_STRIDE + (CMP_BLK - 1)) <= tok) & (n_id < n_cmp)
    s = _dot_nt(q, kc_ref[...]) * scale
    m = jnp.max(jnp.where(valid, s, NEG), axis=-1, keepdims=True)
    e = jnp.where(valid, jnp.exp(s - m), 0.0)
    p = e / jnp.maximum(jnp.sum(e, axis=-1, keepdims=True), TINY)
    oc_ref[...] = _dot(p.astype(BF16), vc_ref[...]).astype(oc_ref.dtype)
    psum = p[0:qt]
    for r in range(1, R):
        psum = psum + p[r * qt:(r + 1) * qt]
    hi, lo = _split(psum)
    ov = ov_ref[...]
    imp = _dot(hi, ov) + _dot(lo, ov)
    t1 = i * qt + lax.broadcasted_iota(jnp.int32, (qt, 1), 0)
    blk = lax.broadcasted_iota(jnp.int32, (1, LANES), 1)
    cur = t1 // SLC_BLK
    forced = (blk == 0) | (blk == cur) | (blk == cur - 1)
    ok = blk * SLC_BLK <= t1
    score = jnp.where(forced, BIG, jnp.where(ok, imp, -BIG))
    score = jnp.where(blk < n_slc, score, -jnp.inf)
    bias = jnp.full((qt, LANES), -BIG, F32)
    for _ in range(slc_k):
        mx = jnp.max(score, axis=-1, keepdims=True)
        idx = jnp.min(jnp.where(score == mx, blk, LANES), axis=-1, keepdims=True)
        hit = blk == idx
        bias = jnp.where(hit, 0.0, bias)
        score = jnp.where(hit, -jnp.inf, score)
    bias_ref[...] = bias.astype(bias_ref.dtype)


def nsa_cmp_select(q_st, kc_c, vc_c, overlap, *, qt, n_cmp, n_slc, slc_k, scale):
    B, G, nQ, rb, d = q_st.shape
    ncp = kc_c.shape[2]
    return pl.pallas_call(
        functools.partial(_cmp_select_kernel, qt=qt, n_cmp=n_cmp, n_slc=n_slc, slc_k=slc_k, scale=scale),
        grid=(B, G, nQ),
        in_specs=[
            pl.BlockSpec((None, None, None, rb, d), lambda b, g, i: (b, g, i, 0, 0)),
            pl.BlockSpec((None, None, ncp, d), lambda b, g, i: (b, g, 0, 0)),
            pl.BlockSpec((None, None, ncp, d), lambda b, g, i: (b, g, 0, 0)),
            pl.BlockSpec((ncp, LANES), lambda b, g, i: (0, 0)),
        ],
        out_specs=[
            pl.BlockSpec((None, None, None, rb, d), lambda b, g, i: (b, g, i, 0, 0)),
            pl.BlockSpec((None, None, None, qt, LANES), lambda b, g, i: (b, g, i, 0, 0)),
        ],
        out_shape=[
            jax.ShapeDtypeStruct((B, G, nQ, rb, d), BF16),
            jax.ShapeDtypeStruct((B, G, nQ, qt, LANES), BF16),
        ],
        compiler_params=_cparams(("arbitrary", "arbitrary", "arbitrary")),
        name="nsa_cmp_select",
    )(q_st, kc_c, vc_c, overlap)


def _gate_combine_kernel(oc_ref, os_ref, ow_ref, gl_ref, o_ref, *, R):
    gl = gl_ref[...]
    gs = jax.nn.sigmoid(gl)
    g = pl.program_id(1)
    for r in range(R):
        outs = []
        acc = None
        for br, ref in enumerate((oc_ref, os_ref, ow_ref)):
            lane = (g * R + r) * 3 + br
            sel = lax.broadcasted_iota(jnp.int32, gs.shape, 1) == lane
            gv = jnp.sum(jnp.where(sel, gs, 0.0), axis=-1, keepdims=True)
            term = gv * ref[r].astype(F32)
            acc = term if acc is None else acc + term
        o_ref[:, r * LANES:(r + 1) * LANES] = acc.astype(o_ref.dtype)


def nsa_gate_combine(oc, osl, ow, gl, *, R):
    nT, G, _, tt, d = oc.shape
    T = nT * tt
    ospec = pl.BlockSpec((None, None, R, tt, d), lambda i, g: (i, g, 0, 0, 0))
    return pl.pallas_call(
        functools.partial(_gate_combine_kernel, R=R),
        grid=(nT, G),
        in_specs=[ospec, ospec, ospec, pl.BlockSpec((tt, LANES), lambda i, g: (i, 0))],
        out_specs=pl.BlockSpec((tt, R * d), lambda i, g: (i, g)),
        out_shape=jax.ShapeDtypeStruct((T, G * R * d), BF16),
        compiler_params=_cparams(("arbitrary", "arbitrary")),
        name="nsa_gate_combine",
    )(oc, osl, ow, gl)


def _topk_rows(s, k):
    n = s.shape[0]
    iota = lax.broadcasted_iota(jnp.int32, s.shape, 0)
    vals, idxs = [], []
    for _ in range(k):
        m = jnp.max(s, axis=0, keepdims=True)
        idx = jnp.min(jnp.where(s == m, iota, n), axis=0, keepdims=True)
        vals.append(m)
        idxs.append(idx)
        s = jnp.where(iota == idx, -jnp.inf, s)
    return jnp.concatenate(vals, axis=0), jnp.concatenate(idxs, axis=0)


def _peer_route_kernel(q_ref, k1_ref, k2_ref, e_ref, g_ref, *, topk, n_keys):
    half = k1_ref.shape[1]
    s1 = _dot3_nt(k1_ref[...], q_ref[:, :half])
    s2 = _dot3_nt(k2_ref[...], q_ref[:, half:])
    v1, i1 = _topk_rows(s1, topk)
    v2, i2 = _topk_rows(s2, topk)
    cand = jnp.concatenate([v1[a:a + 1, :] + v2 for a in range(topk)], axis=0)
    cidx = jnp.concatenate([i1[a:a + 1, :] * n_keys + i2 for a in range(topk)], axis=0)
    top, pos = _topk_rows(cand, topk)
    piota = lax.broadcasted_iota(jnp.int32, cand.shape, 0)
    ex = [jnp.sum(jnp.where(piota == pos[r:r + 1], cidx, 0), axis=0, keepdims=True) for r in range(topk)]
    w = jnp.exp(top - top[0:1])
    e_ref[...] = jnp.concatenate(ex, axis=0)
    g_ref[...] = w / jnp.sum(w, axis=0, keepdims=True)


def peer_route(q, k1, k2, *, heads, topk, tb=128):
    T, N = q.shape
    n_keys, half = k1.shape
    hk = heads * topk
    return pl.pallas_call(
        functools.partial(_peer_route_kernel, topk=topk, n_keys=n_keys),
        grid=(T // tb, heads),
        in_specs=[
            pl.BlockSpec((tb, 2 * half), lambda i, h: (i, h)),
            pl.BlockSpec((n_keys, half), lambda i, h: (0, 0)),
            pl.BlockSpec((n_keys, half), lambda i, h: (0, 0)),
        ],
        out_specs=[pl.BlockSpec((topk, tb), lambda i, h: (h, i)), pl.BlockSpec((topk, tb), lambda i, h: (h, i))],
        out_shape=[jax.ShapeDtypeStruct((hk, T), jnp.int32), jax.ShapeDtypeStruct((hk, T), F32)],
        compiler_params=_cparams(("arbitrary", "arbitrary")),
        name="peer_route",
    )(q, k1, k2)


def _pack_kernel(u_ref, v_ref, o_ref):
    ub = pltpu.bitcast(u_ref[...].astype(BF16).astype(F32), jnp.uint32) >> 16
    vb = pltpu.bitcast(v_ref[...].astype(BF16).astype(F32), jnp.uint32) & jnp.uint32(0xFFFF0000)
    o_ref[...] = vb | ub


def peer_pack(u, v, te=512):
    E, D = u.shape
    spec = pl.BlockSpec((te, D), lambda i: (i, 0))
    return pl.pallas_call(
        _pack_kernel,
        grid=(E // te,),
        in_specs=[spec, spec],
        out_specs=spec,
        out_shape=jax.ShapeDtypeStruct((E, D), jnp.uint32),
        compiler_params=_cparams(("arbitrary",)),
        name="peer_pack",
    )(u, v)


def _peer_mix_kernel(idx_ref, idxn_ref, h_ref, g_ref, x_ref, gate_ref, tbl_ref, o_ref, buf_ref, sem_ref, *, tb, nk):
    i = pl.program_id(0)
    n = pl.num_programs(0)
    rows = tb * nk
    slot = i % 2

    def issue(src_idx_ref, dst_slot):
        def one(r, c):
            e = src_idx_ref[0, r]
            pltpu.make_async_copy(tbl_ref.at[pl.ds(e, 1), :], buf_ref.at[dst_slot, pl.ds(r, 1), :],
                                  sem_ref.at[dst_slot]).start()
            return c
        lax.fori_loop(0, rows, one, 0, unroll=8)

    @pl.when(i == 0)
    def _():
        issue(idx_ref, 0)

    @pl.when(i + 1 < n)
    def _():
        issue(idxn_ref, 1 - slot)

    pltpu.make_async_copy(buf_ref.at[slot], buf_ref.at[slot], sem_ref.at[slot]).wait()

    gt = g_ref[...]
    for t in range(tb):
        w32 = buf_ref[slot, t * nk:(t + 1) * nk, :]
        u = pltpu.bitcast(w32 << 16, F32)
        v = pltpu.bitcast(w32 & jnp.uint32(0xFFFF0000), F32)
        act = jnp.sum(u * h_ref[t:t + 1, :], axis=-1, keepdims=True)
        a = gt[:, t:t + 1] * _gelu_tanh(act)
        y = jnp.sum(a * v, axis=0, keepdims=True)
        o_ref[t:t + 1, :] = x_ref[t:t + 1, :] + gate_ref[...] * y


def peer_mix(experts_t, g_t, h, x, gate, table, *, seq, tb=8):
    T, D = h.shape
    nk = experts_t.shape[0]
    B = gate.shape[0]
    nb = seq // tb
    nblk = T // tb
    idx = experts_t.T.reshape(nblk, 1, tb * nk)
    g = g_t.reshape(nk, nblk, tb).transpose(1, 0, 2)
    smem_spec = lambda f: pl.BlockSpec((None, 1, tb * nk), f, memory_space=pltpu.SMEM)
    return pl.pallas_call(
        functools.partial(_peer_mix_kernel, tb=tb, nk=nk),
        grid=(nblk,),
        in_specs=[
            smem_spec(lambda i: (i, 0, 0)),
            smem_spec(lambda i: (jnp.minimum(i + 1, nblk - 1), 0, 0)),
            pl.BlockSpec((tb, D), lambda i: (i, 0)),
            pl.BlockSpec((None, nk, tb), lambda i: (i, 0, 0)),
            pl.BlockSpec((tb, D), lambda i: (i, 0)),
            pl.BlockSpec((None, 1, D), lambda i: (i // nb, 0, 0)),
            pl.BlockSpec(memory_space=pl.ANY),
        ],
        out_specs=pl.BlockSpec((tb, D), lambda i: (i, 0)),
        out_shape=jax.ShapeDtypeStruct((T, D), F32),
        scratch_shapes=[pltpu.VMEM((2, tb * nk, D), jnp.uint32), pltpu.SemaphoreType.DMA((2,))],
        compiler_params=_cparams(("arbitrary",)),
        name="peer_mix",
    )(idx, idx, h, g, x, gate.reshape(B, 1, D), table)


def _rmsnorm_kernel(x_ref, g_ref, o_ref):
    x = x_ref[...]
    o_ref[...] = (x * lax.rsqrt(jnp.mean(x * x, axis=-1, keepdims=True) + EPS)) * g_ref[...]


def rmsnorm_rows(x, g, tm=512):
    T, D = x.shape
    return pl.pallas_call(
        _rmsnorm_kernel,
        grid=(T // tm,),
        in_specs=[pl.BlockSpec((tm, D), lambda i: (i, 0)), pl.BlockSpec((1, D), lambda i: (0, 0))],
        out_specs=pl.BlockSpec((tm, D), lambda i: (i, 0)),
        out_shape=jax.ShapeDtypeStruct((T, D), F32),
        compiler_params=_cparams(("arbitrary",)),
        name="final_rmsnorm",
    )(x, g.reshape(1, D))


def _split3(m):
    d = m.shape[-1] // 3
    return m[:, :d], m[:, d:2 * d], m[:, 2 * d:]


def sb_mla_layer(x, mod, pos, g_norm, w_in, q_norm, w_uq, kv_norm, w_ukv, w_out, *, batch, seq):
    T, D = x.shape
    shift, scale, gate = _split3(mod)
    sbw = SB_HEADS * SB_DIM
    n_in = w_in.shape[1]
    n_pad = -(-n_in // 512) * 512
    p = norm_matmul(x, g_norm, scale, shift, _pad_cols(w_in, n_pad).astype(BF16), seq=seq, tm=512, tn=512)
    o_a = stick_breaking_attn(p, batch=batch, seq=seq, heads=SB_HEADS, dim=SB_DIM,
                              q_col=0, k_col=SB_HEADS, v_col=2 * SB_HEADS)
    c_q = p[:, 3 * sbw:3 * sbw + MLA_Q_RANK]
    c_kv = p[:, 3 * sbw + MLA_Q_RANK:3 * sbw + MLA_Q_RANK + MLA_KV_RANK]
    k_r = p[:, 3 * sbw + MLA_Q_RANK + MLA_KV_RANK:n_in]
    zq = jnp.zeros((batch, MLA_Q_RANK), F32)
    zkv = jnp.zeros((batch, MLA_KV_RANK), F32)
    dqk = MLA_NOPE + MLA_ROPE
    wq = w_uq.reshape(MLA_Q_RANK, MLA_HEADS, dqk)
    wq = jnp.concatenate([wq[:, :, :MLA_NOPE].reshape(MLA_Q_RANK, -1), wq[:, :, MLA_NOPE:].reshape(MLA_Q_RANK, -1)], 1)
    wkv = w_ukv.reshape(MLA_KV_RANK, MLA_HEADS, MLA_NOPE + MLA_V)
    wkv = jnp.concatenate([wkv[:, :, :MLA_NOPE].reshape(MLA_KV_RANK, -1), wkv[:, :, MLA_NOPE:].reshape(MLA_KV_RANK, -1)], 1)
    qf = norm_matmul(c_q, q_norm, zq, zq, wq.astype(BF16), seq=seq, tm=512, tn=512)
    kvf = norm_matmul(c_kv, kv_norm, zkv, zkv, wkv.astype(BF16), seq=seq, tm=512, tn=512, out_dtype=BF16)
    tables = _rope_tables(pos, MLA_ROPE, MLA_ROPE)
    nope_w = MLA_HEADS * MLA_NOPE
    q_rope = rope_cols(qf, tables, seq=seq, col0=nope_w // (MLA_HEADS * MLA_ROPE), n_heads=MLA_HEADS * MLA_ROPE // LANES,
                       half=MLA_ROPE // 2, tt=512)
    kr_pad = jnp.pad(k_r, ((0, 0), (0, LANES - MLA_ROPE)))
    k_rope = rope_cols(kr_pad, tables, seq=seq, col0=0, n_heads=1, half=MLA_ROPE // 2, tt=512)[:, :MLA_ROPE]
    H = MLA_HEADS
    q_nope = qf[:, :nope_w].astype(BF16).reshape(batch, seq, H, MLA_NOPE)
    q_cat = jnp.concatenate([q_nope, q_rope.reshape(batch, seq, H, MLA_ROPE)], -1)
    tq = 256
    q_cat = q_cat.transpose(0, 2, 1, 3).reshape(batch * H, seq // tq, tq, dqk)
    k_nope = kvf[:, :nope_w].reshape(batch, seq, H, MLA_NOPE)
    k_cat = jnp.concatenate([k_nope, jnp.broadcast_to(k_rope.reshape(batch, seq, 1, MLA_ROPE), (batch, seq, H, MLA_ROPE))], -1)
    k_cat = k_cat.transpose(0, 2, 1, 3).reshape(batch * H, seq, dqk)
    v = kvf[:, nope_w:].reshape(batch, seq, H, MLA_V).transpose(0, 2, 1, 3).reshape(batch * H, seq, MLA_V)
    o_b = flash_attn(q_cat, k_cat, v, qt=tq, tk=256, scale=1.0 / math.sqrt(dqk))
    o_b = o_b.reshape(batch, H, seq, MLA_V).transpose(0, 2, 1, 3).reshape(T, H * MLA_V)
    wo = w_out.astype(BF16)
    return matmul_residual([o_a, o_b], [wo[:sbw], wo[sbw:]], x, gate, seq=seq, tm=512, tn=min(512, D))


def nsa_layer(x, mod, pos, g_norm, w_in, pe_k, pe_v, w1_k, w2_k, w1_v, w2_v, w_out, *, batch, seq):
    T, D = x.shape
    shift, scale, gate = _split3(mod)
    G, R, d = NSA_GROUPS, NSA_HEADS // NSA_GROUPS, NSA_DIM
    H = NSA_HEADS
    kvw = G * d
    n_in = w_in.shape[1]
    n_pad = -(-n_in // 768) * 768
    p = norm_matmul(x, g_norm, scale, shift, _pad_cols(w_in, n_pad).astype(BF16), seq=seq, tm=512, tn=768)
    tables = _rope_tables(pos, ROT_DIM, d)
    half = ROT_DIM // 2
    qt = QBLK
    nQ = seq // qt
    q_st = rope_cols(p, tables, seq=seq, col0=0, n_heads=R, half=half, tt=qt, stacked_groups=G)
    q_st = q_st.reshape(batch, nQ, G, R * qt, d).transpose(0, 2, 1, 3, 4)
    base = H * d // kvw
    kc = rope_cols(p, tables, seq=seq, col0=base + 0, n_heads=G, half=half, tt=512)
    ks = rope_cols(p, tables, seq=seq, col0=base + 2, n_heads=G, half=half, tt=512)
    kw = rope_cols(p, tables, seq=seq, col0=base + 4, n_heads=G, half=half, tt=512)
    off = H * d
    vc = p[:, off + kvw:off + 2 * kvw].astype(BF16)
    vs = p[:, off + 3 * kvw:off + 4 * kvw].astype(BF16)
    vw = p[:, off + 5 * kvw:off + 6 * kvw].astype(BF16)
    gl = jnp.pad(p[:, off + 6 * kvw:n_in], ((0, 0), (0, LANES - 3 * H)))

    def per_group(a):
        return a.reshape(batch, seq, G, d).transpose(0, 2, 1, 3)

    M = seq // CMP_STRIDE
    n_cmp = (seq - CMP_BLK) // CMP_STRIDE + 1
    ncp = -(-M // LANES) * LANES
    def chunks(a):
        c = per_group(a).reshape(batch, G, M, CMP_STRIDE * d)
        return jnp.pad(c, ((0, 0), (0, 0), (0, ncp - M), (0, 0)))
    kc_c = nsa_compress(chunks(kc), pe_k, w1_k, w2_k)
    vc_c = nsa_compress(chunks(vc), pe_v, w1_v, w2_v)
    n_slc = seq // SLC_BLK
    slc_k = min(SLC_TOPK, n_slc)
    c_s = np.arange(ncp) * CMP_STRIDE
    s_s = np.arange(LANES) * SLC_BLK
    ovl = np.clip(np.minimum(c_s[:, None] + CMP_BLK, s_s[None, :] + SLC_BLK) - np.maximum(c_s[:, None], s_s[None, :]), 0, None)
    ovl[n_cmp:, :] = 0
    ovl[:, n_slc:] = 0
    scale_a = 1.0 / math.sqrt(d)
    oc, bias = nsa_cmp_select(q_st, kc_c, vc_c, jnp.asarray(ovl, BF16), qt=qt, n_cmp=n_cmp, n_slc=n_slc,
                              slc_k=slc_k, scale=scale_a)
    q_aug = jnp.concatenate([q_st, jnp.tile(bias, (1, 1, 1, R, 1))], -1).reshape(batch * G, nQ, R * qt, 2 * d)
    onehot = (np.arange(seq)[:, None] // SLC_BLK == np.arange(LANES)[None, :]).astype(np.float32)
    ks_g = per_group(ks)
    k_aug = jnp.concatenate([ks_g, jnp.broadcast_to(jnp.asarray(onehot, BF16), ks_g.shape[:2] + onehot.shape)], -1)
    k_aug = k_aug.reshape(batch * G, seq, 2 * d)
    osl = flash_attn(q_aug, k_aug, per_group(vs).reshape(batch * G, seq, d), qt=qt, tk=256, scale=scale_a)
    q_flat = q_st.reshape(batch * G, nQ, R * qt, d)
    ow = flash_attn(q_flat, per_group(kw).reshape(batch * G, seq, d), per_group(vw).reshape(batch * G, seq, d),
                    qt=qt, tk=128, scale=scale_a, window=WINDOW)

    def unstack(o):
        return o.reshape(batch, G, nQ, R, qt, d).transpose(0, 2, 1, 3, 4, 5).reshape(batch * nQ, G, R, qt, d)

    o = nsa_gate_combine(unstack(oc.reshape(batch * G, nQ, R * qt, d)), unstack(osl), unstack(ow), gl, R=R)
    return matmul_residual([o], [w_out.astype(BF16)], x, gate, seq=seq, tm=512, tn=min(512, D))


def peer_layer(x, mod, g_norm, w_q, k1, k2, u, v, *, batch, seq):
    shift, scale, gate = _split3(mod)
    q, h = norm_matmul(x, g_norm, scale, shift, w_q.astype(BF16), seq=seq, tm=512, tn=512, emit_h=True)
    experts, g = peer_route(q, k1, k2, heads=PEER_HEADS, topk=PEER_TOPK)
    table = peer_pack(u, v)
    return peer_mix(experts, g, h, x, gate, table, seq=seq)


def kernel(x, c, positions, norm_mix, ada_mix_w, ada_mix_b, sbmla_w_in, mla_q_norm, mla_w_uq, mla_kv_norm, mla_w_ukv, sbmla_w_out, nsa_w_in, nsa_pe_k, nsa_pe_v, nsa_w1_k, nsa_w2_k, nsa_w1_v, nsa_w2_v, nsa_w_out, norm_ffn, ada_ffn_w, ada_ffn_b, peer_w_q, peer_k1, peer_k2, peer_u, peer_v, final_norm):
    B, S, D = x.shape
    depth = norm_mix.shape[0]
    mod_mix = modulation_all(c, ada_mix_w, ada_mix_b)
    mod_ffn = modulation_all(c, ada_ffn_w, ada_ffn_b)
    xs = x.reshape(B * S, D)
    for layer in range(depth):
        i = layer // 2
        if layer % 2 == 0:
            xs = sb_mla_layer(xs, mod_mix[layer], positions, norm_mix[layer], sbmla_w_in[i], mla_q_norm[i], mla_w_uq[i],
                              mla_kv_norm[i], mla_w_ukv[i], sbmla_w_out[i], batch=B, seq=S)
        else:
            xs = nsa_layer(xs, mod_mix[layer], positions, norm_mix[layer], nsa_w_in[i], nsa_pe_k[i], nsa_pe_v[i],
                           nsa_w1_k[i], nsa_w2_k[i], nsa_w1_v[i], nsa_w2_v[i], nsa_w_out[i], batch=B, seq=S)
        xs = peer_layer(xs, mod_ffn[layer], norm_ffn[layer], peer_w_q[layer], peer_k1[layer], peer_k2[layer],
                        peer_u[layer], peer_v[layer], batch=B, seq=S)
    return rmsnorm_rows(xs, final_norm).reshape(B, S, D)
```

```python
import functools
import math

import numpy as np
import jax
import jax.numpy as jnp
from jax import lax
from jax.experimental import pallas as pl
from jax.experimental.pallas import tpu as pltpu

F32 = jnp.float32
BF16 = jnp.bfloat16

QBLK = 128
ROPE_THETA = 500000.0
EPS = 1e-6
TINY = 1e-30
BIG = 1e9
NEG = -1e30

SB_HEADS = 8
SB_DIM = 128
MLA_HEADS = 8
MLA_Q_RANK = 512
MLA_KV_RANK = 256
MLA_NOPE = 128
MLA_ROPE = 64
MLA_V = 128
NSA_HEADS = 16
NSA_GROUPS = 2
NSA_DIM = 128
ROT_DIM = NSA_DIM // 4
CMP_BLK = 32
CMP_STRIDE = 16
CMP_HIDDEN = 256
SLC_BLK = 64
SLC_TOPK = 16
WINDOW = 512
PEER_HEADS = 8
PEER_KEYS = 128
PEER_DKEY = 256
PEER_TOPK = 16

LANES = 128
VMEM_LIMIT = 56 * 1024 * 1024


def _cparams(sem):
    return pltpu.CompilerParams(dimension_semantics=sem, vmem_limit_bytes=VMEM_LIMIT)


def _split(a):
    hi = a.astype(BF16)
    lo = (a - hi.astype(F32)).astype(BF16)
    return hi, lo


def _dot(a, b):
    return jnp.dot(a, b, preferred_element_type=F32)


def _dot_nt(a, b):
    return lax.dot_general(a, b, (((1,), (1,)), ((), ())), preferred_element_type=F32)


def _dot3(a, b):
    ah, al = _split(a)
    bh, bl = _split(b)
    return _dot(ah, bh) + (_dot(ah, bl) + _dot(al, bh))


def _dot3_nt(a, b):
    ah, al = _split(a)
    bh, bl = _split(b)
    return _dot_nt(ah, bh) + (_dot_nt(ah, bl) + _dot_nt(al, bh))


def _gelu_tanh(x):
    return 0.5 * x * (1.0 + jnp.tanh(math.sqrt(2.0 / math.pi) * (x + 0.044715 * (x * x * x))))


def _pad_cols(w, n):
    return jnp.pad(w, ((0, 0), (0, n - w.shape[1])))


def _mod_kernel(c_ref, w_ref, b_ref, o_ref):
    c = c_ref[...]
    s = c * jax.nn.sigmoid(c)
    o_ref[0] = _dot3(s, w_ref[0]) + b_ref[0]


def modulation_all(c, w, b):
    L, D, N = w.shape
    B = c.shape[0]
    rows = 8
    cp = jnp.pad(c, ((0, rows - B), (0, 0)))
    tn = 768 if N % 768 == 0 else N
    out = pl.pallas_call(
        _mod_kernel,
        grid=(L, N // tn),
        in_specs=[
            pl.BlockSpec((rows, D), lambda l, j: (0, 0)),
            pl.BlockSpec((1, D, tn), lambda l, j: (l, 0, j)),
            pl.BlockSpec((1, 1, tn), lambda l, j: (l, 0, j)),
        ],
        out_specs=pl.BlockSpec((1, rows, tn), lambda l, j: (l, 0, j)),
        out_shape=jax.ShapeDtypeStruct((L, rows, N), F32),
        compiler_params=_cparams(("arbitrary", "arbitrary")),
        name="adaln_mod",
    )(cp, w, b.reshape(L, 1, N))
    return out[:, :B]


def _norm_mm_kernel(x_ref, g_ref, sc_ref, sh_ref, w_ref, o_ref, *rest, emit_h):
    if emit_h:
        h_ref, hb_ref = rest
    else:
        (hb_ref,) = rest

    @pl.when(pl.program_id(1) == 0)
    def _():
        x = x_ref[...]
        y = x * lax.rsqrt(jnp.mean(x * x, axis=-1, keepdims=True) + EPS)
        h = (y * g_ref[...]) * (1.0 + sc_ref[...]) + sh_ref[...]
        hb_ref[...] = h.astype(BF16)
        if emit_h:
            h_ref[...] = h

    o_ref[...] = _dot(hb_ref[...], w_ref[...]).astype(o_ref.dtype)


def norm_matmul(x, g, scale, shift, w, *, seq, tm, tn, emit_h=False, out_dtype=F32):
    T, K = x.shape
    N = w.shape[1]
    assert T % tm == 0 and N % tn == 0 and seq % tm == 0
    nb = seq // tm
    B = scale.shape[0]
    out_shape = [jax.ShapeDtypeStruct((T, N), out_dtype)]
    out_specs = [pl.BlockSpec((tm, tn), lambda i, j: (i, j))]
    if emit_h:
        out_shape.append(jax.ShapeDtypeStruct((T, K), F32))
        out_specs.append(pl.BlockSpec((tm, K), lambda i, j: (i, 0)))
    res = pl.pallas_call(
        functools.partial(_norm_mm_kernel, emit_h=emit_h),
        grid=(T // tm, N // tn),
        in_specs=[
            pl.BlockSpec((tm, K), lambda i, j: (i, 0)),
            pl.BlockSpec((1, K), lambda i, j: (0, 0)),
            pl.BlockSpec((None, 1, K), lambda i, j: (i // nb, 0, 0)),
            pl.BlockSpec((None, 1, K), lambda i, j: (i // nb, 0, 0)),
            pl.BlockSpec((K, tn), lambda i, j: (0, j)),
        ],
        out_specs=out_specs,
        out_shape=out_shape,
        scratch_shapes=[pltpu.VMEM((tm, K), BF16)],
        compiler_params=_cparams(("arbitrary", "arbitrary")),
        name="norm_matmul",
    )(x, g.reshape(1, K), scale.reshape(B, 1, K), shift.reshape(B, 1, K), w)
    return res if emit_h else res[0]


def _mm_res_kernel(*refs, n_in):
    a_refs = refs[:n_in]
    w_refs = refs[n_in:2 * n_in]
    x_ref, gate_ref, o_ref = refs[2 * n_in:]
    y = _dot(a_refs[0][...], w_refs[0][...])
    for a_ref, w_ref in zip(a_refs[1:], w_refs[1:]):
        y = y + _dot(a_ref[...], w_ref[...])
    o_ref[...] = x_ref[...] + gate_ref[...] * y


def matmul_residual(a_list, w_list, x, gate, *, seq, tm, tn):
    T, D = x.shape
    B = gate.shape[0]
    nb = seq // tm
    n_in = len(a_list)
    in_specs = [pl.BlockSpec((tm, a.shape[1]), lambda i, j: (i, 0)) for a in a_list]
    in_specs += [pl.BlockSpec((w.shape[0], tn), lambda i, j: (0, j)) for w in w_list]
    in_specs += [
        pl.BlockSpec((tm, tn), lambda i, j: (i, j)),
        pl.BlockSpec((None, 1, tn), lambda i, j: (i // nb, 0, j)),
    ]
    return pl.pallas_call(
        functools.partial(_mm_res_kernel, n_in=n_in),
        grid=(T // tm, D // tn),
        in_specs=in_specs,
        out_specs=pl.BlockSpec((tm, tn), lambda i, j: (i, j)),
        out_shape=jax.ShapeDtypeStruct((T, D), F32),
        compiler_params=_cparams(("arbitrary", "arbitrary")),
        name="matmul_residual",
    )(*a_list, *w_list, x, gate.reshape(B, 1, D))


def _sb_kernel(q_ref, k_ref, v_ref, o_ref, c_ref, acc_ref, za_ref, zb_ref, qb_ref, *, tq, scale):
    i = pl.program_id(1)
    tk = tq
    qb_ref[...] = q_ref[...].astype(BF16)
    t_idx = i * tq + lax.broadcasted_iota(jnp.int32, (tq, 1), 0)
    col = lax.broadcasted_iota(jnp.int32, (1, tk), 1)
    tri = (lax.broadcasted_iota(jnp.int32, (tk, tk), 0) > lax.broadcasted_iota(jnp.int32, (tk, tk), 1)).astype(BF16)
    c_ref[...] = jnp.zeros(c_ref.shape, F32)
    acc_ref[...] = jnp.zeros(acc_ref.shape, F32)

    def tile_start(n):
        return pl.multiple_of(jnp.maximum(i - n, 0) * tk, tk)

    def scores(n, z_ref):
        z_ref[...] = _dot_nt(qb_ref[...], k_ref[pl.ds(tile_start(n), tk), :].astype(BF16))

    def update(n, z_ref):
        j = i - n
        v = v_ref[pl.ds(tile_start(n), tk), :].astype(BF16)
        z = z_ref[...] * scale
        sp = jnp.maximum(z, 0.0) + jnp.log(1.0 + jnp.exp(-jnp.abs(z)))
        kpos = jnp.where(j >= 0, j * tk, jnp.int32(2 ** 30)) + col
        strict = kpos < t_idx
        l1m = jnp.where(strict, -sp, 0.0)
        hi, lo = _split(l1m)
        suf = _dot(hi, tri) + _dot(lo, tri)
        c = c_ref[...]
        a = jnp.where(strict, jnp.exp((z - sp) + (suf + c)), 0.0)
        acc_ref[...] += _dot(a.astype(BF16), v)
        c_ref[...] = c + jnp.sum(l1m, axis=-1, keepdims=True)

    scores(0, za_ref)

    def body(m, carry):
        n = 2 * m
        scores(n + 1, zb_ref)
        update(n, za_ref)
        scores(n + 2, za_ref)
        update(n + 1, zb_ref)
        return carry

    lax.fori_loop(0, (i + 2) // 2, body, 0)
    o_ref[...] = acc_ref[...].astype(o_ref.dtype)


def stick_breaking_attn(p, *, batch, seq, heads, dim, q_col, k_col, v_col, tq=256):
    nq = seq // tq
    return pl.pallas_call(
        functools.partial(_sb_kernel, tq=tq, scale=1.0 / math.sqrt(dim)),
        grid=(batch * heads, nq),
        in_specs=[
            pl.BlockSpec((tq, dim), lambda bh, i: ((bh // heads) * nq + i, q_col + bh % heads)),
            pl.BlockSpec((seq, dim), lambda bh, i: (bh // heads, k_col + bh % heads)),
            pl.BlockSpec((seq, dim), lambda bh, i: (bh // heads, v_col + bh % heads)),
        ],
        out_specs=pl.BlockSpec((tq, dim), lambda bh, i: ((bh // heads) * nq + i, bh % heads)),
        out_shape=jax.ShapeDtypeStruct((batch * seq, heads * dim), BF16),
        scratch_shapes=[pltpu.VMEM((tq, 1), F32), pltpu.VMEM((tq, dim), F32), pltpu.VMEM((tq, tq), F32),
                        pltpu.VMEM((tq, tq), F32), pltpu.VMEM((tq, dim), BF16)],
        compiler_params=_cparams(("arbitrary", "arbitrary")),
        name="stick_breaking",
    )(p, p, p)


def _flash_kernel(q_ref, k_ref, v_ref, o_ref, m_ref, l_ref, acc_ref, sa_ref, sb_ref, *, qt, tk, scale, window):
    i = pl.program_id(1)
    rb = q_ref.shape[0]
    q0 = i * qt
    row = lax.broadcasted_iota(jnp.int32, (rb, 1), 0)
    tok = q0 + (row % qt if rb != qt else row)
    col = lax.broadcasted_iota(jnp.int32, (1, tk), 1)
    hi_blk = (q0 + qt - 1) // tk
    lo_blk = 0 if window is None else jnp.maximum(q0 - window + 1, 0) // tk
    m_ref[...] = jnp.full(m_ref.shape, NEG, F32)
    l_ref[...] = jnp.zeros(l_ref.shape, F32)
    acc_ref[...] = jnp.zeros(acc_ref.shape, F32)

    def tile_start(j):
        return pl.multiple_of(jnp.minimum(j, hi_blk) * tk, tk)

    def scores(j, s_ref):
        s_ref[...] = _dot_nt(q_ref[...], k_ref[pl.ds(tile_start(j), tk), :])

    def update(j, s_ref):
        v = v_ref[pl.ds(tile_start(j), tk), :]
        s = s_ref[...] * scale
        kpos = j * tk + col
        mask = kpos <= tok
        if window is not None:
            mask = mask & (kpos > tok - window)
        m_prev = m_ref[...]
        m_new = jnp.maximum(m_prev, jnp.max(jnp.where(mask, s, NEG), axis=-1, keepdims=True))
        p = jnp.where(mask, jnp.exp(s - m_new), 0.0)
        alpha = jnp.exp(m_prev - m_new)
        l_ref[...] = alpha * l_ref[...] + jnp.sum(p, axis=-1, keepdims=True)
        acc_ref[...] = alpha * acc_ref[...] + _dot(p.astype(BF16), v)
        m_ref[...] = m_new

    scores(lo_blk, sa_ref)

    def body(n, carry):
        j = lo_blk + 2 * n
        scores(j + 1, sb_ref)
        update(j, sa_ref)
        scores(j + 2, sa_ref)
        update(j + 1, sb_ref)
        return carry

    lax.fori_loop(0, (hi_blk - lo_blk + 2) // 2, body, 0)
    o_ref[...] = (acc_ref[...] / jnp.maximum(l_ref[...], TINY)).astype(o_ref.dtype)


def flash_attn(q, k, v, *, qt, tk, scale, window=None, kv_rep=1, out_dtype=BF16):
    BH, nQ, rb, dq = q.shape
    S = k.shape[1]
    dv = v.shape[2]
    return pl.pallas_call(
        functools.partial(_flash_kernel, qt=qt, tk=tk, scale=scale, window=window),
        grid=(BH, nQ),
        in_specs=[
            pl.BlockSpec((None, None, rb, dq), lambda bh, i: (bh, i, 0, 0)),
            pl.BlockSpec((None, S, dq), lambda bh, i: (bh // kv_rep, 0, 0)),
            pl.BlockSpec((None, S, dv), lambda bh, i: (bh // kv_rep, 0, 0)),
        ],
        out_specs=pl.BlockSpec((None, None, rb, dv), lambda bh, i: (bh, i, 0, 0)),
        out_shape=jax.ShapeDtypeStruct((BH, nQ, rb, dv), out_dtype),
        scratch_shapes=[pltpu.VMEM((rb, 1), F32), pltpu.VMEM((rb, 1), F32), pltpu.VMEM((rb, dv), F32),
                        pltpu.VMEM((rb, tk), F32), pltpu.VMEM((rb, tk), F32)],
        compiler_params=_cparams(("arbitrary", "arbitrary")),
        name="flash_attn",
    )(q, k, v)


def _rope_tables(pos, rot_dim, period):
    half = rot_dim // 2
    inv = ROPE_THETA ** (-jnp.arange(half, dtype=F32) / half)
    ang = pos.astype(F32)[..., None] * inv
    cos, sin = jnp.cos(ang), jnp.sin(ang)
    rest = period - rot_dim
    shp = cos.shape[:-1]
    c = jnp.concatenate([cos, cos, jnp.ones(shp + (rest,), F32)], -1)
    sn = jnp.concatenate([-sin, jnp.zeros(shp + (half + rest,), F32)], -1)
    sp = jnp.concatenate([jnp.zeros(shp + (half,), F32), sin, jnp.zeros(shp + (rest,), F32)], -1)
    rep = LANES // period
    return tuple(jnp.tile(t, (1, 1, rep)) for t in (c, sn, sp))


def _rope_lanes(x, c, sn, sp, half):
    return x * c + pltpu.roll(x, LANES - half, 1) * sn + pltpu.roll(x, half, 1) * sp


def _rope_kernel(x_ref, c_ref, sn_ref, sp_ref, o_ref, *, half, n_heads, stacked):
    c, sn, sp = c_ref[...], sn_ref[...], sp_ref[...]
    for h in range(n_heads):
        y = _rope_lanes(x_ref[:, h * LANES:(h + 1) * LANES], c, sn, sp, half).astype(o_ref.dtype)
        if stacked:
            o_ref[h] = y
        else:
            o_ref[:, h * LANES:(h + 1) * LANES] = y


def rope_cols(p, tables, *, seq, col0, n_heads, half, tt, stacked_groups=None):
    T = p.shape[0]
    nb = seq // tt
    width = n_heads * LANES
    tspec = pl.BlockSpec((None, tt, LANES), lambda i, g: (i // nb, i % nb, 0))
    if stacked_groups is None:
        grid = (T // tt, 1)
        out_spec = pl.BlockSpec((tt, width), lambda i, g: (i, 0))
        out_shape = jax.ShapeDtypeStruct((T, width), BF16)
    else:
        grid = (T // tt, stacked_groups)
        out_spec = pl.BlockSpec((None, None, n_heads, tt, LANES), lambda i, g: (i, g, 0, 0, 0))
        out_shape = jax.ShapeDtypeStruct((T // tt, stacked_groups, n_heads, tt, LANES), BF16)
    return pl.pallas_call(
        functools.partial(_rope_kernel, half=half, n_heads=n_heads, stacked=stacked_groups is not None),
        grid=grid,
        in_specs=[pl.BlockSpec((tt, width), lambda i, g: (i, col0 + g)), tspec, tspec, tspec],
        out_specs=out_spec,
        out_shape=out_shape,
        compiler_params=_cparams(("arbitrary", "arbitrary")),
        name="rope",
    )(p, *tables)


def _compress_kernel(t_ref, pe_ref, w1_ref, w2_ref, o_ref):
    t = t_ref[...]
    half = t.shape[1]
    w1a = w1_ref[:half, :]
    w1b = w1_ref[half:, :]
    pe = pe_ref[...]
    bias = _dot3(jnp.broadcast_to(pe[:, :half], (8, half)), w1a.astype(F32)) + _dot3(
        jnp.broadcast_to(pe[:, half:], (8, half)), w1b.astype(F32))
    a = _dot(t, w1a)
    b = _dot(t, w1b)
    n = a.shape[0]
    hid = a + pltpu.roll(b, n - 1, 0) + bias[0:1, :]
    o_ref[...] = _dot(_gelu_tanh(hid).astype(BF16), w2_ref[...]).astype(o_ref.dtype)


def nsa_compress(t2, pe, w1, w2):
    B, G, M, K = t2.shape
    d = w2.shape[1]
    return pl.pallas_call(
        _compress_kernel,
        grid=(B, G),
        in_specs=[
            pl.BlockSpec((None, None, M, K), lambda b, g: (b, g, 0, 0)),
            pl.BlockSpec((1, 2 * K), lambda b, g: (0, 0)),
            pl.BlockSpec((2 * K, w1.shape[1]), lambda b, g: (0, 0)),
            pl.BlockSpec(w2.shape, lambda b, g: (0, 0)),
        ],
        out_specs=pl.BlockSpec((None, None, M, d), lambda b, g: (b, g, 0, 0)),
        out_shape=jax.ShapeDtypeStruct((B, G, M, d), BF16),
        compiler_params=_cparams(("arbitrary", "arbitrary")),
        name="nsa_compress",
    )(t2, pe.reshape(1, 2 * K), w1.astype(BF16), w2.astype(BF16))


def _cmp_select_kernel(q_ref, kc_ref, vc_ref, ov_ref, oc_ref, bias_ref, *, qt, n_cmp, n_slc, slc_k, scale):
    i = pl.program_id(2)
    rb = q_ref.shape[0]
    R = rb // qt
    ncp = kc_ref.shape[0]
    q = q_ref[...]
    tok = i * qt + lax.broadcasted_iota(jnp.int32, (rb, 1), 0) % qt
    n_id = lax.broadcasted_iota(jnp.int32, (1, ncp), 1)
    valid = ((n_id * CMP_STRIDE + (CMP_BLK - 1)) <= tok) & (n_id < n_cmp)
    s = _dot_nt(q, kc_ref[...]) * scale
    m = jnp.max(jnp.where(valid, s, NEG), axis=-1, keepdims=True)
    e = jnp.where(valid, jnp.exp(s - m), 0.0)
    p = e / jnp.maximum(jnp.sum(e, axis=-1, keepdims=True), TINY)
    oc_ref[...] = _dot(p.astype(BF16), vc_ref[...]).astype(oc_ref.dtype)
    psum = p[0:qt]
    for r in range(1, R):
        psum = psum + p[r * qt:(r + 1) * qt]
    hi, lo = _split(psum)
    ov = ov_ref[...]
    imp = _dot(hi, ov) + _dot(lo, ov)
    t1 = i * qt + lax.broadcasted_iota(jnp.int32, (qt, 1), 0)
    blk = lax.broadcasted_iota(jnp.int32, (1, LANES), 1)
    cur = t1 // SLC_BLK
    forced = (blk == 0) | (blk == cur) | (blk == cur - 1)
    ok = blk * SLC_BLK <= t1
    score = jnp.where(forced, BIG, jnp.where(ok, imp, -BIG))
    score = jnp.where(blk < n_slc, score, -jnp.inf)
    bias = jnp.full((qt, LANES), -BIG, F32)
    for _ in range(slc_k):
        mx = jnp.max(score, axis=-1, keepdims=True)
        idx = jnp.min(jnp.where(score == mx, blk, LANES), axis=-1, keepdims=True)
        hit = blk == idx
        bias = jnp.where(hit, 0.0, bias)
        score = jnp.where(hit, -jnp.inf, score)
    bias_ref[...] = bias.astype(bias_ref.dtype)


def nsa_cmp_select(q_st, kc_c, vc_c, overlap, *, qt, n_cmp, n_slc, slc_k, scale):
    B, G, nQ, rb, d = q_st.shape
    ncp = kc_c.shape[2]
    return pl.pallas_call(
        functools.partial(_cmp_select_kernel, qt=qt, n_cmp=n_cmp, n_slc=n_slc, slc_k=slc_k, scale=scale),
        grid=(B, G, nQ),
        in_specs=[
            pl.BlockSpec((None, None, None, rb, d), lambda b, g, i: (b, g, i, 0, 0)),
            pl.BlockSpec((None, None, ncp, d), lambda b, g, i: (b, g, 0, 0)),
            pl.BlockSpec((None, None, ncp, d), lambda b, g, i: (b, g, 0, 0)),
            pl.BlockSpec((ncp, LANES), lambda b, g, i: (0, 0)),
        ],
        out_specs=[
            pl.BlockSpec((None, None, None, rb, d), lambda b, g, i: (b, g, i, 0, 0)),
            pl.BlockSpec((None, None, None, qt, LANES), lambda b, g, i: (b, g, i, 0, 0)),
        ],
        out_shape=[
            jax.ShapeDtypeStruct((B, G, nQ, rb, d), BF16),
            jax.ShapeDtypeStruct((B, G, nQ, qt, LANES), BF16),
        ],
        compiler_params=_cparams(("arbitrary", "arbitrary", "arbitrary")),
        name="nsa_cmp_select",
    )(q_st, kc_c, vc_c, overlap)


def _gate_combine_kernel(oc_ref, os_ref, ow_ref, gl_ref, o_ref, *, R):
    gl = gl_ref[...]
    gs = jax.nn.sigmoid(gl)
    g = pl.program_id(1)
    for r in range(R):
        outs = []
        acc = None
        for br, ref in enumerate((oc_ref, os_ref, ow_ref)):
            lane = (g * R + r) * 3 + br
            sel = lax.broadcasted_iota(jnp.int32, gs.shape, 1) == lane
            gv = jnp.sum(jnp.where(sel, gs, 0.0), axis=-1, keepdims=True)
            term = gv * ref[r].astype(F32)
            acc = term if acc is None else acc + term
        o_ref[:, r * LANES:(r + 1) * LANES] = acc.astype(o_ref.dtype)


def nsa_gate_combine(oc, osl, ow, gl, *, R):
    nT, G, _, tt, d = oc.shape
    T = nT * tt
    ospec = pl.BlockSpec((None, None, R, tt, d), lambda i, g: (i, g, 0, 0, 0))
    return pl.pallas_call(
        functools.partial(_gate_combine_kernel, R=R),
        grid=(nT, G),
        in_specs=[ospec, ospec, ospec, pl.BlockSpec((tt, LANES), lambda i, g: (i, 0))],
        out_specs=pl.BlockSpec((tt, R * d), lambda i, g: (i, g)),
        out_shape=jax.ShapeDtypeStruct((T, G * R * d), BF16),
        compiler_params=_cparams(("arbitrary", "arbitrary")),
        name="nsa_gate_combine",
    )(oc, osl, ow, gl)


def _topk_rows(s, k):
    n = s.shape[0]
    iota = lax.broadcasted_iota(jnp.int32, s.shape, 0)
    vals, idxs = [], []
    for _ in range(k):
        m = jnp.max(s, axis=0, keepdims=True)
        idx = jnp.min(jnp.where(s == m, iota, n), axis=0, keepdims=True)
        vals.append(m)
        idxs.append(idx)
        s = jnp.where(iota == idx, -jnp.inf, s)
    return jnp.concatenate(vals, axis=0), jnp.concatenate(idxs, axis=0)


def _peer_route_kernel(q_ref, k1_ref, k2_ref, e_ref, g_ref, *, topk, n_keys):
    half = k1_ref.shape[1]
    s1 = _dot3_nt(k1_ref[...], q_ref[:, :half])
    s2 = _dot3_nt(k2_ref[...], q_ref[:, half:])
    v1, i1 = _topk_rows(s1, topk)
    v2, i2 = _topk_rows(s2, topk)
    cand = jnp.concatenate([v1[a:a + 1, :] + v2 for a in range(topk)], axis=0)
    cidx = jnp.concatenate([i1[a:a + 1, :] * n_keys + i2 for a in range(topk)], axis=0)
    top, pos = _topk_rows(cand, topk)
    piota = lax.broadcasted_iota(jnp.int32, cand.shape, 0)
    ex = [jnp.sum(jnp.where(piota == pos[r:r + 1], cidx, 0), axis=0, keepdims=True) for r in range(topk)]
    w = jnp.exp(top - top[0:1])
    e_ref[...] = jnp.concatenate(ex, axis=0)
    g_ref[...] = w / jnp.sum(w, axis=0, keepdims=True)


def peer_route(q, k1, k2, *, heads, topk, tb=128):
    T, N = q.shape
    n_keys, half = k1.shape
    hk = heads * topk
    return pl.pallas_call(
        functools.partial(_peer_route_kernel, topk=topk, n_keys=n_keys),
        grid=(T // tb, heads),
        in_specs=[
            pl.BlockSpec((tb, 2 * half), lambda i, h: (i, h)),
            pl.BlockSpec((n_keys, half), lambda i, h: (0, 0)),
            pl.BlockSpec((n_keys, half), lambda i, h: (0, 0)),
        ],
        out_specs=[pl.BlockSpec((topk, tb), lambda i, h: (h, i)), pl.BlockSpec((topk, tb), lambda i, h: (h, i))],
        out_shape=[jax.ShapeDtypeStruct((hk, T), jnp.int32), jax.ShapeDtypeStruct((hk, T), F32)],
        compiler_params=_cparams(("arbitrary", "arbitrary")),
        name="peer_route",
    )(q, k1, k2)


def _pack_kernel(u_ref, v_ref, o_ref):
    ub = pltpu.bitcast(u_ref[...].astype(BF16).astype(F32), jnp.uint32) >> 16
    vb = pltpu.bitcast(v_ref[...].astype(BF16).astype(F32), jnp.uint32) & jnp.uint32(0xFFFF0000)
    o_ref[...] = vb | ub


def peer_pack(u, v, te=512):
    E, D = u.shape
    spec = pl.BlockSpec((te, D), lambda i: (i, 0))
    return pl.pallas_call(
        _pack_kernel,
        grid=(E // te,),
        in_specs=[spec, spec],
        out_specs=spec,
        out_shape=jax.ShapeDtypeStruct((E, D), jnp.uint32),
        compiler_params=_cparams(("arbitrary",)),
        name="peer_pack",
    )(u, v)


SUBLANES = 8


def _peer_mix_kernel(idx_ref, idxn_ref, h_ref, g_ref, x_ref, gate_ref, tbl_ref, o_ref, buf_ref, sem_ref, *, tb, nk):
    i = pl.program_id(0)
    n = pl.num_programs(0)
    rows = tb * nk
    nlt = buf_ref.shape[1]
    slot = i % 2
    hi_mask = jnp.uint32(0xFFFF0000)

    def row_copy(src_idx_ref, r, dst_slot):
        return pltpu.make_async_copy(tbl_ref.at[src_idx_ref[0, r]], buf_ref.at[dst_slot, :, r, :], sem_ref.at[dst_slot])

    def slot_copy(s):
        return pltpu.make_async_copy(buf_ref.at[s], buf_ref.at[s], sem_ref.at[s])

    @pl.when(i == 0)
    def _():
        def one(r, c):
            row_copy(idx_ref, r, 0).start()
            return c
        lax.fori_loop(0, rows, one, 0, unroll=8)

    slot_copy(slot).wait()

    ys = []
    for t in range(tb):
        hb = [jnp.broadcast_to(h_ref[t:t + 1, lt * LANES:(lt + 1) * LANES], (SUBLANES, LANES)) for lt in range(nlt)]
        yacc = [None] * nlt
        for c in range(nk // SUBLANES):
            r0 = t * nk + c * SUBLANES
            ws = [buf_ref[slot, lt, r0:r0 + SUBLANES, :] for lt in range(nlt)]
            s = None
            for lt in range(nlt):
                term = pltpu.bitcast(ws[lt] << 16, F32) * hb[lt]
                s = term if s is None else s + term
            act = jnp.sum(s, axis=-1, keepdims=True)
            a = g_ref[c * SUBLANES:(c + 1) * SUBLANES, t:t + 1] * _gelu_tanh(act)
            for lt in range(nlt):
                yv = a * pltpu.bitcast(ws[lt] & hi_mask, F32)
                yacc[lt] = yv if yacc[lt] is None else yacc[lt] + yv
            for r in range(r0, r0 + SUBLANES):
                row_copy(idxn_ref, r, 1 - slot).start(priority=r % 2)
        ys.append(jnp.concatenate([jnp.sum(ya, axis=0, keepdims=True) for ya in yacc], axis=1))
    o_ref[...] = x_ref[...] + gate_ref[...] * jnp.concatenate(ys, axis=0)

    @pl.when(i == n - 1)
    def _():
        slot_copy(1 - slot).wait()


def peer_mix(experts_t, g_t, h, x, gate, table, *, seq, tb=8):
    T, D = h.shape
    nk = experts_t.shape[0]
    B = gate.shape[0]
    nb = seq // tb
    nblk = T // tb
    idx = experts_t.T.reshape(nblk, 1, tb * nk)
    g = g_t.reshape(nk, nblk, tb).transpose(1, 0, 2)
    smem_spec = lambda f: pl.BlockSpec((None, 1, tb * nk), f, memory_space=pltpu.SMEM)
    return pl.pallas_call(
        functools.partial(_peer_mix_kernel, tb=tb, nk=nk),
        grid=(nblk,),
        in_specs=[
            smem_spec(lambda i: (i, 0, 0)),
            smem_spec(lambda i: (jnp.minimum(i + 1, nblk - 1), 0, 0)),
            pl.BlockSpec((tb, D), lambda i: (i, 0)),
            pl.BlockSpec((None, nk, tb), lambda i: (i, 0, 0)),
            pl.BlockSpec((tb, D), lambda i: (i, 0)),
            pl.BlockSpec((None, 1, D), lambda i: (i // nb, 0, 0)),
            pl.BlockSpec(memory_space=pl.ANY),
        ],
        out_specs=pl.BlockSpec((tb, D), lambda i: (i, 0)),
        out_shape=jax.ShapeDtypeStruct((T, D), F32),
        scratch_shapes=[pltpu.VMEM((2, D // LANES, tb * nk, LANES), jnp.uint32), pltpu.SemaphoreType.DMA((2,))],
        compiler_params=_cparams(("arbitrary",)),
        name="peer_mix",
    )(idx, idx, h, g, x, gate.reshape(B, 1, D), table)


def _rmsnorm_kernel(x_ref, g_ref, o_ref):
    x = x_ref[...]
    o_ref[...] = (x * lax.rsqrt(jnp.mean(x * x, axis=-1, keepdims=True) + EPS)) * g_ref[...]


def rmsnorm_rows(x, g, tm=512):
    T, D = x.shape
    return pl.pallas_call(
        _rmsnorm_kernel,
        grid=(T // tm,),
        in_specs=[pl.BlockSpec((tm, D), lambda i: (i, 0)), pl.BlockSpec((1, D), lambda i: (0, 0))],
        out_specs=pl.BlockSpec((tm, D), lambda i: (i, 0)),
        out_shape=jax.ShapeDtypeStruct((T, D), F32),
        compiler_params=_cparams(("arbitrary",)),
        name="final_rmsnorm",
    )(x, g.reshape(1, D))


def _split3(m):
    d = m.shape[-1] // 3
    return m[:, :d], m[:, d:2 * d], m[:, 2 * d:]


def sb_mla_layer(x, mod, pos, g_norm, w_in, q_norm, w_uq, kv_norm, w_ukv, w_out, *, batch, seq):
    T, D = x.shape
    shift, scale, gate = _split3(mod)
    sbw = SB_HEADS * SB_DIM
    n_in = w_in.shape[1]
    n_pad = -(-n_in // 512) * 512
    p = norm_matmul(x, g_norm, scale, shift, _pad_cols(w_in, n_pad).astype(BF16), seq=seq, tm=512, tn=512)
    o_a = stick_breaking_attn(p, batch=batch, seq=seq, heads=SB_HEADS, dim=SB_DIM,
                              q_col=0, k_col=SB_HEADS, v_col=2 * SB_HEADS)
    c_q = p[:, 3 * sbw:3 * sbw + MLA_Q_RANK]
    c_kv = p[:, 3 * sbw + MLA_Q_RANK:3 * sbw + MLA_Q_RANK + MLA_KV_RANK]
    k_r = p[:, 3 * sbw + MLA_Q_RANK + MLA_KV_RANK:n_in]
    zq = jnp.zeros((batch, MLA_Q_RANK), F32)
    zkv = jnp.zeros((batch, MLA_KV_RANK), F32)
    dqk = MLA_NOPE + MLA_ROPE
    wq = w_uq.reshape(MLA_Q_RANK, MLA_HEADS, dqk)
    wq = jnp.concatenate([wq[:, :, :MLA_NOPE].reshape(MLA_Q_RANK, -1), wq[:, :, MLA_NOPE:].reshape(MLA_Q_RANK, -1)], 1)
    wkv = w_ukv.reshape(MLA_KV_RANK, MLA_HEADS, MLA_NOPE + MLA_V)
    wkv = jnp.concatenate([wkv[:, :, :MLA_NOPE].reshape(MLA_KV_RANK, -1), wkv[:, :, MLA_NOPE:].reshape(MLA_KV_RANK, -1)], 1)
    qf = norm_matmul(c_q, q_norm, zq, zq, wq.astype(BF16), seq=seq, tm=512, tn=512)
    kvf = norm_matmul(c_kv, kv_norm, zkv, zkv, wkv.astype(BF16), seq=seq, tm=512, tn=512, out_dtype=BF16)
    tables = _rope_tables(pos, MLA_ROPE, MLA_ROPE)
    nope_w = MLA_HEADS * MLA_NOPE
    q_rope = rope_cols(qf, tables, seq=seq, col0=nope_w // (MLA_HEADS * MLA_ROPE), n_heads=MLA_HEADS * MLA_ROPE // LANES,
                       half=MLA_ROPE // 2, tt=512)
    kr_pad = jnp.pad(k_r, ((0, 0), (0, LANES - MLA_ROPE)))
    k_rope = rope_cols(kr_pad, tables, seq=seq, col0=0, n_heads=1, half=MLA_ROPE // 2, tt=512)[:, :MLA_ROPE]
    H = MLA_HEADS
    q_nope = qf[:, :nope_w].astype(BF16).reshape(batch, seq, H, MLA_NOPE)
    q_cat = jnp.concatenate([q_nope, q_rope.reshape(batch, seq, H, MLA_ROPE)], -1)
    tq = 256
    q_cat = q_cat.transpose(0, 2, 1, 3).reshape(batch * H, seq // tq, tq, dqk)
    k_nope = kvf[:, :nope_w].reshape(batch, seq, H, MLA_NOPE)
    k_cat = jnp.concatenate([k_nope, jnp.broadcast_to(k_rope.reshape(batch, seq, 1, MLA_ROPE), (batch, seq, H, MLA_ROPE))], -1)
    k_cat = k_cat.transpose(0, 2, 1, 3).reshape(batch * H, seq, dqk)
    v = kvf[:, nope_w:].reshape(batch, seq, H, MLA_V).transpose(0, 2, 1, 3).reshape(batch * H, seq, MLA_V)
    o_b = flash_attn(q_cat, k_cat, v, qt=tq, tk=256, scale=1.0 / math.sqrt(dqk))
    o_b = o_b.reshape(batch, H, seq, MLA_V).transpose(0, 2, 1, 3).reshape(T, H * MLA_V)
    wo = w_out.astype(BF16)
    return matmul_residual([o_a, o_b], [wo[:sbw], wo[sbw:]], x, gate, seq=seq, tm=512, tn=min(512, D))


def nsa_layer(x, mod, pos, g_norm, w_in, pe_k, pe_v, w1_k, w2_k, w1_v, w2_v, w_out, *, batch, seq):
    T, D = x.shape
    shift, scale, gate = _split3(mod)
    G, R, d = NSA_GROUPS, NSA_HEADS // NSA_GROUPS, NSA_DIM
    H = NSA_HEADS
    kvw = G * d
    n_in = w_in.shape[1]
    n_pad = -(-n_in // 768) * 768
    p = norm_matmul(x, g_norm, scale, shift, _pad_cols(w_in, n_pad).astype(BF16), seq=seq, tm=512, tn=768)
    tables = _rope_tables(pos, ROT_DIM, d)
    half = ROT_DIM // 2
    qt = QBLK
    nQ = seq // qt
    q_st = rope_cols(p, tables, seq=seq, col0=0, n_heads=R, half=half, tt=qt, stacked_groups=G)
    q_st = q_st.reshape(batch, nQ, G, R * qt, d).transpose(0, 2, 1, 3, 4)
    base = H * d // kvw
    kc = rope_cols(p, tables, seq=seq, col0=base + 0, n_heads=G, half=half, tt=512)
    ks = rope_cols(p, tables, seq=seq, col0=base + 2, n_heads=G, half=half, tt=512)
    kw = rope_cols(p, tables, seq=seq, col0=base + 4, n_heads=G, half=half, tt=512)
    off = H * d
    vc = p[:, off + kvw:off + 2 * kvw].astype(BF16)
    vs = p[:, off + 3 * kvw:off + 4 * kvw].astype(BF16)
    vw = p[:, off + 5 * kvw:off + 6 * kvw].astype(BF16)
    gl = jnp.pad(p[:, off + 6 * kvw:n_in], ((0, 0), (0, LANES - 3 * H)))

    def per_group(a):
        return a.reshape(batch, seq, G, d).transpose(0, 2, 1, 3)

    M = seq // CMP_STRIDE
    n_cmp = (seq - CMP_BLK) // CMP_STRIDE + 1
    ncp = -(-M // LANES) * LANES
    def chunks(a):
        c = per_group(a).reshape(batch, G, M, CMP_STRIDE * d)
        return jnp.pad(c, ((0, 0), (0, 0), (0, ncp - M), (0, 0)))
    kc_c = nsa_compress(chunks(kc), pe_k, w1_k, w2_k)
    vc_c = nsa_compress(chunks(vc), pe_v, w1_v, w2_v)
    n_slc = seq // SLC_BLK
    slc_k = min(SLC_TOPK, n_slc)
    c_s = np.arange(ncp) * CMP_STRIDE
    s_s = np.arange(LANES) * SLC_BLK
    ovl = np.clip(np.minimum(c_s[:, None] + CMP_BLK, s_s[None, :] + SLC_BLK) - np.maximum(c_s[:, None], s_s[None, :]), 0, None)
    ovl[n_cmp:, :] = 0
    ovl[:, n_slc:] = 0
    scale_a = 1.0 / math.sqrt(d)
    oc, bias = nsa_cmp_select(q_st, kc_c, vc_c, jnp.asarray(ovl, BF16), qt=qt, n_cmp=n_cmp, n_slc=n_slc,
                              slc_k=slc_k, scale=scale_a)
    q_aug = jnp.concatenate([q_st, jnp.tile(bias, (1, 1, 1, R, 1))], -1).reshape(batch * G, nQ, R * qt, 2 * d)
    onehot = (np.arange(seq)[:, None] // SLC_BLK == np.arange(LANES)[None, :]).astype(np.float32)
    ks_g = per_group(ks)
    k_aug = jnp.concatenate([ks_g, jnp.broadcast_to(jnp.asarray(onehot, BF16), ks_g.shape[:2] + onehot.shape)], -1)
    k_aug = k_aug.reshape(batch * G, seq, 2 * d)
    osl = flash_attn(q_aug, k_aug, per_group(vs).reshape(batch * G, seq, d), qt=qt, tk=256, scale=scale_a)
    q_flat = q_st.reshape(batch * G, nQ, R * qt, d)
    ow = flash_attn(q_flat, per_group(kw).reshape(batch * G, seq, d), per_group(vw).reshape(batch * G, seq, d),
                    qt=qt, tk=128, scale=scale_a, window=WINDOW)

    def unstack(o):
        return o.reshape(batch, G, nQ, R, qt, d).transpose(0, 2, 1, 3, 4, 5).reshape(batch * nQ, G, R, qt, d)

    o = nsa_gate_combine(unstack(oc.reshape(batch * G, nQ, R * qt, d)), unstack(osl), unstack(ow), gl, R=R)
    return matmul_residual([o], [w_out.astype(BF16)], x, gate, seq=seq, tm=512, tn=min(512, D))


def peer_layer(x, mod, g_norm, w_q, k1, k2, u, v, *, batch, seq):
    shift, scale, gate = _split3(mod)
    q, h = norm_matmul(x, g_norm, scale, shift, w_q.astype(BF16), seq=seq, tm=512, tn=512, emit_h=True)
    experts, g = peer_route(q, k1, k2, heads=PEER_HEADS, topk=PEER_TOPK)
    table = peer_pack(u, v).reshape(u.shape[0], u.shape[1] // LANES, LANES)
    return peer_mix(experts, g, h, x, gate, table, seq=seq)


def kernel(x, c, positions, norm_mix, ada_mix_w, ada_mix_b, sbmla_w_in, mla_q_norm, mla_w_uq, mla_kv_norm, mla_w_ukv, sbmla_w_out, nsa_w_in, nsa_pe_k, nsa_pe_v, nsa_w1_k, nsa_w2_k, nsa_w1_v, nsa_w2_v, nsa_w_out, norm_ffn, ada_ffn_w, ada_ffn_b, peer_w_q, peer_k1, peer_k2, peer_u, peer_v, final_norm):
    B, S, D = x.shape
    depth = norm_mix.shape[0]
    mod_mix = modulation_all(c, ada_mix_w, ada_mix_b)
    mod_ffn = modulation_all(c, ada_ffn_w, ada_ffn_b)
    xs = x.reshape(B * S, D)
    for layer in range(depth):
        i = layer // 2
        if layer % 2 == 0:
            xs = sb_mla_layer(xs, mod_mix[layer], positions, norm_mix[layer], sbmla_w_in[i], mla_q_norm[i], mla_w_uq[i],
                              mla_kv_norm[i], mla_w_ukv[i], sbmla_w_out[i], batch=B, seq=S)
        else:
            xs = nsa_layer(xs, mod_mix[layer], positions, norm_mix[layer], nsa_w_in[i], nsa_pe_k[i], nsa_pe_v[i],
                           nsa_w1_k[i], nsa_w2_k[i], nsa_w1_v[i], nsa_w2_v[i], nsa_w_out[i], batch=B, seq=S)
        xs = peer_layer(xs, mod_ffn[layer], norm_ffn[layer], peer_w_q[layer], peer_k1[layer], peer_k2[layer],
                        peer_u[layer], peer_v[layer], batch=B, seq=S)
    return rmsnorm_rows(xs, final_norm).reshape(B, S, D)
```

```python
import functools
import math

import numpy as np
import jax
import jax.numpy as jnp
from jax import lax
from jax.experimental import pallas as pl
from jax.experimental.pallas import tpu as pltpu

F32 = jnp.float32
BF16 = jnp.bfloat16

QBLK = 128
ROPE_THETA = 500000.0
EPS = 1e-6
TINY = 1e-30
BIG = 1e9
NEG = -1e30

SB_HEADS = 8
SB_DIM = 128
MLA_HEADS = 8
MLA_Q_RANK = 512
MLA_KV_RANK = 256
MLA_NOPE = 128
MLA_ROPE = 64
MLA_V = 128
NSA_HEADS = 16
NSA_GROUPS = 2
NSA_DIM = 128
ROT_DIM = NSA_DIM // 4
CMP_BLK = 32
CMP_STRIDE = 16
CMP_HIDDEN = 256
SLC_BLK = 64
SLC_TOPK = 16
WINDOW = 512
PEER_HEADS = 8
PEER_KEYS = 128
PEER_DKEY = 256
PEER_TOPK = 16

LANES = 128
VMEM_LIMIT = 56 * 1024 * 1024


def _cparams(sem):
    return pltpu.CompilerParams(dimension_semantics=sem, vmem_limit_bytes=VMEM_LIMIT)


def _split(a):
    hi = a.astype(BF16)
    lo = (a - hi.astype(F32)).astype(BF16)
    return hi, lo


def _dot(a, b):
    return jnp.dot(a, b, preferred_element_type=F32)


def _dot_nt(a, b):
    return lax.dot_general(a, b, (((1,), (1,)), ((), ())), preferred_element_type=F32)


def _dot3(a, b):
    ah, al = _split(a)
    bh, bl = _split(b)
    return _dot(ah, bh) + (_dot(ah, bl) + _dot(al, bh))


def _dot3_nt(a, b):
    ah, al = _split(a)
    bh, bl = _split(b)
    return _dot_nt(ah, bh) + (_dot_nt(ah, bl) + _dot_nt(al, bh))


def _gelu_tanh(x):
    return 0.5 * x * (1.0 + jnp.tanh(math.sqrt(2.0 / math.pi) * (x + 0.044715 * (x * x * x))))


def _pad_cols(w, n):
    return jnp.pad(w, ((0, 0), (0, n - w.shape[1])))


def _mod_kernel(c_ref, w_ref, b_ref, o_ref):
    c = c_ref[...]
    s = c * jax.nn.sigmoid(c)
    o_ref[0] = _dot3(s, w_ref[0]) + b_ref[0]


def modulation_all(c, w, b):
    L, D, N = w.shape
    B = c.shape[0]
    rows = 8
    cp = jnp.pad(c, ((0, rows - B), (0, 0)))
    tn = 768 if N % 768 == 0 else N
    out = pl.pallas_call(
        _mod_kernel,
        grid=(L, N // tn),
        in_specs=[
            pl.BlockSpec((rows, D), lambda l, j: (0, 0)),
            pl.BlockSpec((1, D, tn), lambda l, j: (l, 0, j)),
            pl.BlockSpec((1, 1, tn), lambda l, j: (l, 0, j)),
        ],
        out_specs=pl.BlockSpec((1, rows, tn), lambda l, j: (l, 0, j)),
        out_shape=jax.ShapeDtypeStruct((L, rows, N), F32),
        compiler_params=_cparams(("arbitrary", "arbitrary")),
        name="adaln_mod",
    )(cp, w, b.reshape(L, 1, N))
    return out[:, :B]


def _norm_mm_kernel(x_ref, g_ref, sc_ref, sh_ref, w_ref, o_ref, *rest, emit_h):
    if emit_h:
        h_ref, hb_ref = rest
    else:
        (hb_ref,) = rest

    @pl.when(pl.program_id(1) == 0)
    def _():
        x = x_ref[...]
        y = x * lax.rsqrt(jnp.mean(x * x, axis=-1, keepdims=True) + EPS)
        h = (y * g_ref[...]) * (1.0 + sc_ref[...]) + sh_ref[...]
        hb_ref[...] = h.astype(BF16)
        if emit_h:
            h_ref[...] = h

    o_ref[...] = _dot(hb_ref[...], w_ref[...]).astype(o_ref.dtype)


def norm_matmul(x, g, scale, shift, w, *, seq, tm, tn, emit_h=False, out_dtype=F32):
    T, K = x.shape
    N = w.shape[1]
    assert T % tm == 0 and N % tn == 0 and seq % tm == 0
    nb = seq // tm
    B = scale.shape[0]
    out_shape = [jax.ShapeDtypeStruct((T, N), out_dtype)]
    out_specs = [pl.BlockSpec((tm, tn), lambda i, j: (i, j))]
    if emit_h:
        out_shape.append(jax.ShapeDtypeStruct((T, K), F32))
        out_specs.append(pl.BlockSpec((tm, K), lambda i, j: (i, 0)))
    res = pl.pallas_call(
        functools.partial(_norm_mm_kernel, emit_h=emit_h),
        grid=(T // tm, N // tn),
        in_specs=[
            pl.BlockSpec((tm, K), lambda i, j: (i, 0)),
            pl.BlockSpec((1, K), lambda i, j: (0, 0)),
            pl.BlockSpec((None, 1, K), lambda i, j: (i // nb, 0, 0)),
            pl.BlockSpec((None, 1, K), lambda i, j: (i // nb, 0, 0)),
            pl.BlockSpec((K, tn), lambda i, j: (0, j)),
        ],
        out_specs=out_specs,
        out_shape=out_shape,
        scratch_shapes=[pltpu.VMEM((tm, K), BF16)],
        compiler_params=_cparams(("arbitrary", "arbitrary")),
        name="norm_matmul",
    )(x, g.reshape(1, K), scale.reshape(B, 1, K), shift.reshape(B, 1, K), w)
    return res if emit_h else res[0]


def _mm_res_kernel(*refs, n_in):
    a_refs = refs[:n_in]
    w_refs = refs[n_in:2 * n_in]
    x_ref, gate_ref, o_ref = refs[2 * n_in:]
    y = _dot(a_refs[0][...], w_refs[0][...])
    for a_ref, w_ref in zip(a_refs[1:], w_refs[1:]):
        y = y + _dot(a_ref[...], w_ref[...])
    o_ref[...] = x_ref[...] + gate_ref[...] * y


def matmul_residual(a_list, w_list, x, gate, *, seq, tm, tn):
    T, D = x.shape
    B = gate.shape[0]
    nb = seq // tm
    n_in = len(a_list)
    in_specs = [pl.BlockSpec((tm, a.shape[1]), lambda i, j: (i, 0)) for a in a_list]
    in_specs += [pl.BlockSpec((w.shape[0], tn), lambda i, j: (0, j)) for w in w_list]
    in_specs += [
        pl.BlockSpec((tm, tn), lambda i, j: (i, j)),
        pl.BlockSpec((None, 1, tn), lambda i, j: (i // nb, 0, j)),
    ]
    return pl.pallas_call(
        functools.partial(_mm_res_kernel, n_in=n_in),
        grid=(T // tm, D // tn),
        in_specs=in_specs,
        out_specs=pl.BlockSpec((tm, tn), lambda i, j: (i, j)),
        out_shape=jax.ShapeDtypeStruct((T, D), F32),
        compiler_params=_cparams(("arbitrary", "arbitrary")),
        name="matmul_residual",
    )(*a_list, *w_list, x, gate.reshape(B, 1, D))


LOG2E = 1.4426950408889634


def _sb_kernel(q_ref, k_ref, v_ref, o_ref, c_ref, acc_ref, za_ref, zb_ref, qb_ref, *, tq, tk, scale):
    i = pl.program_id(1)
    qb_ref[...] = q_ref[...].astype(BF16)
    t_idx = i * tq + lax.broadcasted_iota(jnp.int32, (tq, 1), 0)
    col = lax.broadcasted_iota(jnp.int32, (1, tk), 1)
    tri = (lax.broadcasted_iota(jnp.int32, (tk, tk), 0) > lax.broadcasted_iota(jnp.int32, (tk, tk), 1)).astype(BF16)
    c_ref[...] = jnp.zeros(c_ref.shape, F32)
    acc_ref[...] = jnp.zeros(acc_ref.shape, F32)
    top = 2 * i + 1

    def tile_start(n):
        return pl.multiple_of(jnp.maximum(top - n, 0) * tk, tk)

    def scores(n, z_ref):
        z_ref[...] = _dot_nt(qb_ref[...], k_ref[pl.ds(tile_start(n), tk), :].astype(BF16))

    def update(n, z_ref, masked):
        v = v_ref[pl.ds(tile_start(n), tk), :].astype(BF16)
        z = z_ref[...] * (scale * LOG2E)
        sp = jnp.maximum(z, 0.0) + jnp.log2(1.0 + jnp.exp2(-jnp.abs(z)))
        if masked:
            strict = ((top - n) * tk + col) < t_idx
            l1m = jnp.where(strict, -sp, 0.0)
        else:
            l1m = -sp
        hi, lo = _split(l1m)
        suf = _dot(hi, tri) + _dot(lo, tri)
        c = c_ref[...]
        a = jnp.exp2((z - sp) + (suf + c))
        if masked:
            a = jnp.where(strict, a, 0.0)
        acc_ref[...] += _dot(a.astype(BF16), v)
        c_ref[...] = c + jnp.sum(l1m, axis=-1, keepdims=True)

    scores(0, za_ref)
    scores(1, zb_ref)
    update(0, za_ref, True)
    scores(2, za_ref)
    update(1, zb_ref, True)

    def body(m, carry):
        n = 2 * m
        scores(n + 1, zb_ref)
        update(n, za_ref, False)
        scores(n + 2, za_ref)
        update(n + 1, zb_ref, False)
        return carry

    lax.fori_loop(1, i + 1, body, 0)
    o_ref[...] = acc_ref[...].astype(o_ref.dtype)


def stick_breaking_attn(p, *, batch, seq, heads, dim, q_col, k_col, v_col, tq=512):
    nq = seq // tq
    tk = tq // 2
    return pl.pallas_call(
        functools.partial(_sb_kernel, tq=tq, tk=tk, scale=1.0 / math.sqrt(dim)),
        grid=(batch * heads, nq),
        in_specs=[
            pl.BlockSpec((tq, dim), lambda bh, i: ((bh // heads) * nq + i, q_col + bh % heads)),
            pl.BlockSpec((seq, dim), lambda bh, i: (bh // heads, k_col + bh % heads)),
            pl.BlockSpec((seq, dim), lambda bh, i: (bh // heads, v_col + bh % heads)),
        ],
        out_specs=pl.BlockSpec((tq, dim), lambda bh, i: ((bh // heads) * nq + i, bh % heads)),
        out_shape=jax.ShapeDtypeStruct((batch * seq, heads * dim), BF16),
        scratch_shapes=[pltpu.VMEM((tq, 1), F32), pltpu.VMEM((tq, dim), F32), pltpu.VMEM((tq, tk), F32),
                        pltpu.VMEM((tq, tk), F32), pltpu.VMEM((tq, dim), BF16)],
        compiler_params=_cparams(("arbitrary", "arbitrary")),
        name="stick_breaking",
    )(p, p, p)


FLASH_ROW_BLOCK = 128


def _flash_kernel(q_ref, k_ref, v_ref, o_ref, m_ref, acc_ref, sa_ref, sb_ref, p_ref, alpha_ref, *, qt, tk, scale, window):
    i = pl.program_id(1)
    rb = q_ref.shape[0]
    dv = v_ref.shape[1]
    nch = tk // LANES
    q0 = i * qt
    row = lax.broadcasted_iota(jnp.int32, (rb, 1), 0)
    tok = q0 + (row % qt if rb != qt else row)
    col = lax.broadcasted_iota(jnp.int32, (1, LANES), 1)
    hi_blk = (q0 + qt - 1) // tk
    lo_blk = 0 if window is None else jnp.maximum(q0 - window + 1, 0) // tk
    m_ref[...] = jnp.full(m_ref.shape, NEG, F32)
    acc_ref[...] = jnp.zeros(acc_ref.shape, F32)
    ones = jnp.ones((tk, LANES), BF16)

    def tile_start(j):
        return pl.multiple_of(jnp.minimum(j, hi_blk) * tk, tk)

    def scores(j, s_ref):
        s_ref[...] = _dot_nt(q_ref[...], k_ref[pl.ds(tile_start(j), tk), :])

    def update(j, s_ref, masked):
        v_aug = jnp.concatenate([v_ref[pl.ds(tile_start(j), tk), :], ones], axis=1)
        for r0 in range(0, rb, FLASH_ROW_BLOCK):
            rs = slice(r0, r0 + FLASH_ROW_BLOCK)
            chunks = [s_ref[rs, c * LANES:(c + 1) * LANES] * (scale * LOG2E) for c in range(nch)]
            if masked:
                masks = []
                for c in range(nch):
                    kpos = j * tk + c * LANES + col
                    mk = kpos <= tok[rs]
                    if window is not None:
                        mk = mk & (kpos > tok[rs] - window)
                    masks.append(mk)
                chunks = [jnp.where(mk, ch, NEG) for mk, ch in zip(masks, chunks)]
            cmax = chunks[0]
            for ch in chunks[1:]:
                cmax = jnp.maximum(cmax, ch)
            m_prev = m_ref[rs, :]
            m_new = jnp.maximum(m_prev, jnp.max(cmax, axis=-1, keepdims=True))
            ps = [jnp.exp2(ch - m_new) for ch in chunks]
            if masked:
                ps = [jnp.where(mk, p, 0.0) for mk, p in zip(masks, ps)]
            p_ref[rs, :] = jnp.concatenate(ps, axis=1).astype(BF16)
            alpha_ref[rs, :] = jnp.exp2(m_prev - m_new)
            m_ref[rs, :] = m_new
        pv = _dot(p_ref[...], v_aug)
        alpha = alpha_ref[...]
        for c in range((dv + LANES) // LANES):
            sl = slice(c * LANES, (c + 1) * LANES)
            acc_ref[:, sl] = alpha * acc_ref[:, sl] + pv[:, sl]

    scores(lo_blk, sa_ref)

    def pair(j, masked):
        scores(j + 1, sb_ref)
        update(j, sa_ref, masked)
        scores(j + 2, sa_ref)
        update(j + 1, sb_ref, masked)

    n_free = 0 if window is not None else ((q0 + 1) // tk) // 2

    def free_body(n, carry):
        pair(lo_blk + 2 * n, False)
        return carry

    def masked_body(n, carry):
        pair(lo_blk + 2 * n, True)
        return carry

    if window is None:
        lax.fori_loop(0, n_free, free_body, 0)
    lax.fori_loop(n_free, (hi_blk - lo_blk + 2) // 2, masked_body, 0)
    o_ref[...] = (acc_ref[:, :dv] / jnp.maximum(acc_ref[:, dv:], TINY)).astype(o_ref.dtype)


def flash_attn(q, k, v, *, qt, tk, scale, window=None, kv_rep=1, out_dtype=BF16):
    BH, nQ, rb, dq = q.shape
    S = k.shape[1]
    dv = v.shape[2]
    assert dv == LANES and tk % LANES == 0
    return pl.pallas_call(
        functools.partial(_flash_kernel, qt=qt, tk=tk, scale=scale, window=window),
        grid=(BH, nQ),
        in_specs=[
            pl.BlockSpec((None, None, rb, dq), lambda bh, i: (bh, i, 0, 0)),
            pl.BlockSpec((None, S, dq), lambda bh, i: (bh // kv_rep, 0, 0)),
            pl.BlockSpec((None, S, dv), lambda bh, i: (bh // kv_rep, 0, 0)),
        ],
        out_specs=pl.BlockSpec((None, None, rb, dv), lambda bh, i: (bh, i, 0, 0)),
        out_shape=jax.ShapeDtypeStruct((BH, nQ, rb, dv), out_dtype),
        scratch_shapes=[pltpu.VMEM((rb, LANES), F32), pltpu.VMEM((rb, dv + LANES), F32),
                        pltpu.VMEM((rb, tk), F32), pltpu.VMEM((rb, tk), F32),
                        pltpu.VMEM((rb, tk), BF16), pltpu.VMEM((rb, LANES), F32)],
        compiler_params=_cparams(("arbitrary", "arbitrary")),
        name="flash_attn",
    )(q, k, v)


def _rope_tables(pos, rot_dim, period):
    half = rot_dim // 2
    inv = ROPE_THETA ** (-jnp.arange(half, dtype=F32) / half)
    ang = pos.astype(F32)[..., None] * inv
    cos, sin = jnp.cos(ang), jnp.sin(ang)
    rest = period - rot_dim
    shp = cos.shape[:-1]
    c = jnp.concatenate([cos, cos, jnp.ones(shp + (rest,), F32)], -1)
    sn = jnp.concatenate([-sin, jnp.zeros(shp + (half + rest,), F32)], -1)
    sp = jnp.concatenate([jnp.zeros(shp + (half,), F32), sin, jnp.zeros(shp + (rest,), F32)], -1)
    rep = LANES // period
    return tuple(jnp.tile(t, (1, 1, rep)) for t in (c, sn, sp))


def _rope_lanes(x, c, sn, sp, half):
    return x * c + pltpu.roll(x, LANES - half, 1) * sn + pltpu.roll(x, half, 1) * sp


def _rope_kernel(x_ref, c_ref, sn_ref, sp_ref, o_ref, *, half, n_heads, stacked):
    c, sn, sp = c_ref[...], sn_ref[...], sp_ref[...]
    for h in range(n_heads):
        y = _rope_lanes(x_ref[:, h * LANES:(h + 1) * LANES], c, sn, sp, half).astype(o_ref.dtype)
        if stacked:
            o_ref[h] = y
        else:
            o_ref[:, h * LANES:(h + 1) * LANES] = y


def rope_cols(p, tables, *, seq, col0, n_heads, half, tt, stacked_groups=None):
    T = p.shape[0]
    nb = seq // tt
    width = n_heads * LANES
    tspec = pl.BlockSpec((None, tt, LANES), lambda i, g: (i // nb, i % nb, 0))
    if stacked_groups is None:
        grid = (T // tt, 1)
        out_spec = pl.BlockSpec((tt, width), lambda i, g: (i, 0))
        out_shape = jax.ShapeDtypeStruct((T, width), BF16)
    else:
        grid = (T // tt, stacked_groups)
        out_spec = pl.BlockSpec((None, None, n_heads, tt, LANES), lambda i, g: (i, g, 0, 0, 0))
        out_shape = jax.ShapeDtypeStruct((T // tt, stacked_groups, n_heads, tt, LANES), BF16)
    return pl.pallas_call(
        functools.partial(_rope_kernel, half=half, n_heads=n_heads, stacked=stacked_groups is not None),
        grid=grid,
        in_specs=[pl.BlockSpec((tt, width), lambda i, g: (i, col0 + g)), tspec, tspec, tspec],
        out_specs=out_spec,
        out_shape=out_shape,
        compiler_params=_cparams(("arbitrary", "arbitrary")),
        name="rope",
    )(p, *tables)


def _compress_kernel(t_ref, pe_ref, w1_ref, w2_ref, o_ref):
    t = t_ref[...]
    half = t.shape[1]
    w1a = w1_ref[:half, :]
    w1b = w1_ref[half:, :]
    pe = pe_ref[...]
    bias = _dot3(jnp.broadcast_to(pe[:, :half], (8, half)), w1a.astype(F32)) + _dot3(
        jnp.broadcast_to(pe[:, half:], (8, half)), w1b.astype(F32))
    a = _dot(t, w1a)
    b = _dot(t, w1b)
    n = a.shape[0]
    hid = a + pltpu.roll(b, n - 1, 0) + bias[0:1, :]
    o_ref[...] = _dot(_gelu_tanh(hid).astype(BF16), w2_ref[...]).astype(o_ref.dtype)


def nsa_compress(t2, pe, w1, w2):
    B, G, M, K = t2.shape
    d = w2.shape[1]
    return pl.pallas_call(
        _compress_kernel,
        grid=(B, G),
        in_specs=[
            pl.BlockSpec((None, None, M, K), lambda b, g: (b, g, 0, 0)),
            pl.BlockSpec((1, 2 * K), lambda b, g: (0, 0)),
            pl.BlockSpec((2 * K, w1.shape[1]), lambda b, g: (0, 0)),
            pl.BlockSpec(w2.shape, lambda b, g: (0, 0)),
        ],
        out_specs=pl.BlockSpec((None, None, M, d), lambda b, g: (b, g, 0, 0)),
        out_shape=jax.ShapeDtypeStruct((B, G, M, d), BF16),
        compiler_params=_cparams(("arbitrary", "arbitrary")),
        name="nsa_compress",
    )(t2, pe.reshape(1, 2 * K), w1.astype(BF16), w2.astype(BF16))


def _cmp_select_kernel(q_ref, kc_ref, vc_ref, ov_ref, oc_ref, bias_ref, *, qt, n_cmp, n_slc, slc_k, scale):
    i = pl.program_id(2)
    rb = q_ref.shape[0]
    R = rb // qt
    ncp = kc_ref.shape[0]
    q = q_ref[...]
    tok = i * qt + lax.broadcasted_iota(jnp.int32, (rb, 1), 0) % qt
    n_id = lax.broadcasted_iota(jnp.int32, (1, ncp), 1)
    valid = ((n_id * CMP_STRIDE + (CMP_BLK - 1)) <= tok) & (n_id < n_cmp)
    s = _dot_nt(q, kc_ref[...]) * scale
    m = jnp.max(jnp.where(valid, s, NEG), axis=-1, keepdims=True)
    e = jnp.where(valid, jnp.exp(s - m), 0.0)
    p = e / jnp.maximum(jnp.sum(e, axis=-1, keepdims=True), TINY)
    oc_ref[...] = _dot(p.astype(BF16), vc_ref[...]).astype(oc_ref.dtype)
    psum = p[0:qt]
    for r in range(1, R):
        psum = psum + p[r * qt:(r + 1) * qt]
    hi, lo = _split(psum)
    ov = ov_ref[...]
    imp = _dot(hi, ov) + _dot(lo, ov)
    t1 = i * qt + lax.broadcasted_iota(jnp.int32, (qt, 1), 0)
    blk = lax.broadcasted_iota(jnp.int32, (1, LANES), 1)
    cur = t1 // SLC_BLK
    forced = (blk == 0) | (blk == cur) | (blk == cur - 1)
    ok = blk * SLC_BLK <= t1
    score = jnp.where(forced, BIG, jnp.where(ok, imp, -BIG))
    score = jnp.where(blk < n_slc, score, -jnp.inf)
    bias = jnp.full((qt, LANES), -BIG, F32)
    for _ in range(slc_k):
        mx = jnp.max(score, axis=-1, keepdims=True)
        idx = jnp.min(jnp.where(score == mx, blk, LANES), axis=-1, keepdims=True)
        hit = blk == idx
        bias = jnp.where(hit, 0.0, bias)
        score = jnp.where(hit, -jnp.inf, score)
    bias_ref[...] = bias.astype(bias_ref.dtype)


def nsa_cmp_select(q_st, kc_c, vc_c, overlap, *, qt, n_cmp, n_slc, slc_k, scale):
    B, G, nQ, rb, d = q_st.shape
    ncp = kc_c.shape[2]
    return pl.pallas_call(
        functools.partial(_cmp_select_kernel, qt=qt, n_cmp=n_cmp, n_slc=n_slc, slc_k=slc_k, scale=scale),
        grid=(B, G, nQ),
        in_specs=[
            pl.BlockSpec((None, None, None, rb, d), lambda b, g, i: (b, g, i, 0, 0)),
            pl.BlockSpec((None, None, ncp, d), lambda b, g, i: (b, g, 0, 0)),
            pl.BlockSpec((None, None, ncp, d), lambda b, g, i: (b, g, 0, 0)),
            pl.BlockSpec((ncp, LANES), lambda b, g, i: (0, 0)),
        ],
        out_specs=[
            pl.BlockSpec((None, None, None, rb, d), lambda b, g, i: (b, g, i, 0, 0)),
            pl.BlockSpec((None, None, None, qt, LANES), lambda b, g, i: (b, g, i, 0, 0)),
        ],
        out_shape=[
            jax.ShapeDtypeStruct((B, G, nQ, rb, d), BF16),
            jax.ShapeDtypeStruct((B, G, nQ, qt, LANES), BF16),
        ],
        compiler_params=_cparams(("arbitrary", "arbitrary", "arbitrary")),
        name="nsa_cmp_select",
    )(q_st, kc_c, vc_c, overlap)


def _gate_combine_kernel(oc_ref, os_ref, ow_ref, gl_ref, o_ref, *, R):
    gl = gl_ref[...]
    gs = jax.nn.sigmoid(gl)
    g = pl.program_id(1)
    for r in range(R):
        outs = []
        acc = None
        for br, ref in enumerate((oc_ref, os_ref, ow_ref)):
            lane = (g * R + r) * 3 + br
            sel = lax.broadcasted_iota(jnp.int32, gs.shape, 1) == lane
            gv = jnp.sum(jnp.where(sel, gs, 0.0), axis=-1, keepdims=True)
            term = gv * ref[r].astype(F32)
            acc = term if acc is None else acc + term
        o_ref[:, r * LANES:(r + 1) * LANES] = acc.astype(o_ref.dtype)


def nsa_gate_combine(oc, osl, ow, gl, *, R):
    nT, G, _, tt, d = oc.shape
    T = nT * tt
    ospec = pl.BlockSpec((None, None, R, tt, d), lambda i, g: (i, g, 0, 0, 0))
    return pl.pallas_call(
        functools.partial(_gate_combine_kernel, R=R),
        grid=(nT, G),
        in_specs=[ospec, ospec, ospec, pl.BlockSpec((tt, LANES), lambda i, g: (i, 0))],
        out_specs=pl.BlockSpec((tt, R * d), lambda i, g: (i, g)),
        out_shape=jax.ShapeDtypeStruct((T, G * R * d), BF16),
        compiler_params=_cparams(("arbitrary", "arbitrary")),
        name="nsa_gate_combine",
    )(oc, osl, ow, gl)


def _topk_rows(s, k):
    n = s.shape[0]
    iota = lax.broadcasted_iota(jnp.int32, s.shape, 0)
    vals, idxs = [], []
    for _ in range(k):
        m = jnp.max(s, axis=0, keepdims=True)
        idx = jnp.min(jnp.where(s == m, iota, n), axis=0, keepdims=True)
        vals.append(m)
        idxs.append(idx)
        s = jnp.where(iota == idx, -jnp.inf, s)
    return jnp.concatenate(vals, axis=0), jnp.concatenate(idxs, axis=0)


def _peer_route_kernel(q_ref, k1_ref, k2_ref, e_ref, g_ref, *, topk, n_keys):
    half = k1_ref.shape[1]
    s1 = _dot3_nt(k1_ref[...], q_ref[:, :half])
    s2 = _dot3_nt(k2_ref[...], q_ref[:, half:])
    v1, i1 = _topk_rows(s1, topk)
    v2, i2 = _topk_rows(s2, topk)
    assert topk == 2 * SUBLANES
    tb = s1.shape[1]
    sub = lax.broadcasted_iota(jnp.int32, (SUBLANES, tb), 0)
    cand, cidx, cpos = [], [], []
    for a in range(SUBLANES):
        for m in range(2 if a == 0 else 1):
            bs = slice(m * SUBLANES, (m + 1) * SUBLANES)
            cand.append(v1[a:a + 1, :] + v2[bs, :])
            cidx.append(i1[a:a + 1, :] * n_keys + i2[bs, :])
            cpos.append(a * topk + m * SUBLANES + sub)
    cand.append(v1[SUBLANES:, :] + v2[0:1, :])
    cidx.append(i1[SUBLANES:, :] * n_keys + i2[0:1, :])
    cpos.append((SUBLANES + sub) * topk)
    cand = jnp.concatenate(cand, axis=0)
    cidx = jnp.concatenate(cidx, axis=0)
    cpos = jnp.concatenate(cpos, axis=0)
    tops, exs = [], []
    for _ in range(topk):
        m = jnp.max(cand, axis=0, keepdims=True)
        p = jnp.min(jnp.where(cand == m, cpos, topk * topk), axis=0, keepdims=True)
        hit = cpos == p
        tops.append(m)
        exs.append(jnp.sum(jnp.where(hit, cidx, 0), axis=0, keepdims=True))
        cand = jnp.where(hit, -jnp.inf, cand)
    top = jnp.concatenate(tops, axis=0)
    w = jnp.exp(top - top[0:1])
    e_ref[...] = jnp.concatenate(exs, axis=0)
    g_ref[...] = w / jnp.sum(w, axis=0, keepdims=True)


def peer_route(q, k1, k2, *, heads, topk, tb=128):
    T, N = q.shape
    n_keys, half = k1.shape
    hk = heads * topk
    return pl.pallas_call(
        functools.partial(_peer_route_kernel, topk=topk, n_keys=n_keys),
        grid=(T // tb, heads),
        in_specs=[
            pl.BlockSpec((tb, 2 * half), lambda i, h: (i, h)),
            pl.BlockSpec((n_keys, half), lambda i, h: (0, 0)),
            pl.BlockSpec((n_keys, half), lambda i, h: (0, 0)),
        ],
        out_specs=[pl.BlockSpec((topk, tb), lambda i, h: (h, i)), pl.BlockSpec((topk, tb), lambda i, h: (h, i))],
        out_shape=[jax.ShapeDtypeStruct((hk, T), jnp.int32), jax.ShapeDtypeStruct((hk, T), F32)],
        compiler_params=_cparams(("arbitrary", "arbitrary")),
        name="peer_route",
    )(q, k1, k2)


def _pack_kernel(u_ref, v_ref, o_ref):
    ub = pltpu.bitcast(u_ref[...].astype(BF16).astype(F32), jnp.uint32) >> 16
    vb = pltpu.bitcast(v_ref[...].astype(BF16).astype(F32), jnp.uint32) & jnp.uint32(0xFFFF0000)
    o_ref[...] = vb | ub


def peer_pack(u, v, te=512):
    E, D = u.shape
    spec = pl.BlockSpec((te, D), lambda i: (i, 0))
    return pl.pallas_call(
        _pack_kernel,
        grid=(E // te,),
        in_specs=[spec, spec],
        out_specs=spec,
        out_shape=jax.ShapeDtypeStruct((E, D), jnp.uint32),
        compiler_params=_cparams(("arbitrary",)),
        name="peer_pack",
    )(u, v)


SUBLANES = 8


def _peer_mix_kernel(idx_ref, idxn_ref, h_ref, g_ref, x_ref, gate_ref, tbl_ref, o_ref, buf_ref, sem_ref, *, tb, nk):
    i = pl.program_id(0)
    n = pl.num_programs(0)
    rows = tb * nk
    nlt = buf_ref.shape[1]
    slot = i % 2
    hi_mask = jnp.uint32(0xFFFF0000)

    def row_copy(src_idx_ref, r, dst_slot):
        return pltpu.make_async_copy(tbl_ref.at[src_idx_ref[0, r]], buf_ref.at[dst_slot, :, r, :], sem_ref.at[dst_slot])

    def slot_copy(s):
        return pltpu.make_async_copy(buf_ref.at[s], buf_ref.at[s], sem_ref.at[s])

    @pl.when(i == 0)
    def _():
        def one(r, c):
            row_copy(idx_ref, r, 0).start()
            return c
        lax.fori_loop(0, rows, one, 0, unroll=8)

    slot_copy(slot).wait()

    ys = []
    for t in range(tb):
        hb = [jnp.broadcast_to(h_ref[t:t + 1, lt * LANES:(lt + 1) * LANES], (SUBLANES, LANES)) for lt in range(nlt)]
        yacc = [None] * nlt
        for c in range(nk // SUBLANES):
            r0 = t * nk + c * SUBLANES
            ws = [buf_ref[slot, lt, r0:r0 + SUBLANES, :] for lt in range(nlt)]
            s = None
            for lt in range(nlt):
                term = pltpu.bitcast(ws[lt] << 16, F32) * hb[lt]
                s = term if s is None else s + term
            act = jnp.sum(s, axis=-1, keepdims=True)
            a = g_ref[c * SUBLANES:(c + 1) * SUBLANES, t:t + 1] * _gelu_tanh(act)
            for lt in range(nlt):
                yv = a * pltpu.bitcast(ws[lt] & hi_mask, F32)
                yacc[lt] = yv if yacc[lt] is None else yacc[lt] + yv
            for r in range(r0, r0 + SUBLANES):
                row_copy(idxn_ref, r, 1 - slot).start(priority=r % 2)
        ys.append(jnp.concatenate([jnp.sum(ya, axis=0, keepdims=True) for ya in yacc], axis=1))
    o_ref[...] = x_ref[...] + gate_ref[...] * jnp.concatenate(ys, axis=0)

    @pl.when(i == n - 1)
    def _():
        slot_copy(1 - slot).wait()


def peer_mix(experts_t, g_t, h, x, gate, table, *, seq, tb=8):
    T, D = h.shape
    nk = experts_t.shape[0]
    B = gate.shape[0]
    nb = seq // tb
    nblk = T // tb
    idx = experts_t.T.reshape(nblk, 1, tb * nk)
    g = g_t.reshape(nk, nblk, tb).transpose(1, 0, 2)
    smem_spec = lambda f: pl.BlockSpec((None, 1, tb * nk), f, memory_space=pltpu.SMEM)
    return pl.pallas_call(
        functools.partial(_peer_mix_kernel, tb=tb, nk=nk),
        grid=(nblk,),
        in_specs=[
            smem_spec(lambda i: (i, 0, 0)),
            smem_spec(lambda i: (jnp.minimum(i + 1, nblk - 1), 0, 0)),
            pl.BlockSpec((tb, D), lambda i: (i, 0)),
            pl.BlockSpec((None, nk, tb), lambda i: (i, 0, 0)),
            pl.BlockSpec((tb, D), lambda i: (i, 0)),
            pl.BlockSpec((None, 1, D), lambda i: (i // nb, 0, 0)),
            pl.BlockSpec(memory_space=pl.ANY),
        ],
        out_specs=pl.BlockSpec((tb, D), lambda i: (i, 0)),
        out_shape=jax.ShapeDtypeStruct((T, D), F32),
        scratch_shapes=[pltpu.VMEM((2, D // LANES, tb * nk, LANES), jnp.uint32), pltpu.SemaphoreType.DMA((2,))],
        compiler_params=_cparams(("arbitrary",)),
        name="peer_mix",
    )(idx, idx, h, g, x, gate.reshape(B, 1, D), table)


def _rmsnorm_kernel(x_ref, g_ref, o_ref):
    x = x_ref[...]
    o_ref[...] = (x * lax.rsqrt(jnp.mean(x * x, axis=-1, keepdims=True) + EPS)) * g_ref[...]


def rmsnorm_rows(x, g, tm=512):
    T, D = x.shape
    return pl.pallas_call(
        _rmsnorm_kernel,
        grid=(T // tm,),
        in_specs=[pl.BlockSpec((tm, D), lambda i: (i, 0)), pl.BlockSpec((1, D), lambda i: (0, 0))],
        out_specs=pl.BlockSpec((tm, D), lambda i: (i, 0)),
        out_shape=jax.ShapeDtypeStruct((T, D), F32),
        compiler_params=_cparams(("arbitrary",)),
        name="final_rmsnorm",
    )(x, g.reshape(1, D))


def _split3(m):
    d = m.shape[-1] // 3
    return m[:, :d], m[:, d:2 * d], m[:, 2 * d:]


def sb_mla_layer(x, mod, pos, g_norm, w_in, q_norm, w_uq, kv_norm, w_ukv, w_out, *, batch, seq):
    T, D = x.shape
    shift, scale, gate = _split3(mod)
    sbw = SB_HEADS * SB_DIM
    n_in = w_in.shape[1]
    n_pad = -(-n_in // 512) * 512
    p = norm_matmul(x, g_norm, scale, shift, _pad_cols(w_in, n_pad).astype(BF16), seq=seq, tm=512, tn=512)
    o_a = stick_breaking_attn(p, batch=batch, seq=seq, heads=SB_HEADS, dim=SB_DIM,
                              q_col=0, k_col=SB_HEADS, v_col=2 * SB_HEADS)
    c_q = p[:, 3 * sbw:3 * sbw + MLA_Q_RANK]
    c_kv = p[:, 3 * sbw + MLA_Q_RANK:3 * sbw + MLA_Q_RANK + MLA_KV_RANK]
    k_r = p[:, 3 * sbw + MLA_Q_RANK + MLA_KV_RANK:n_in]
    zq = jnp.zeros((batch, MLA_Q_RANK), F32)
    zkv = jnp.zeros((batch, MLA_KV_RANK), F32)
    dqk = MLA_NOPE + MLA_ROPE
    wq = w_uq.reshape(MLA_Q_RANK, MLA_HEADS, dqk)
    wq = jnp.concatenate([wq[:, :, :MLA_NOPE].reshape(MLA_Q_RANK, -1), wq[:, :, MLA_NOPE:].reshape(MLA_Q_RANK, -1)], 1)
    wkv = w_ukv.reshape(MLA_KV_RANK, MLA_HEADS, MLA_NOPE + MLA_V)
    wkv = jnp.concatenate([wkv[:, :, :MLA_NOPE].reshape(MLA_KV_RANK, -1), wkv[:, :, MLA_NOPE:].reshape(MLA_KV_RANK, -1)], 1)
    qf = norm_matmul(c_q, q_norm, zq, zq, wq.astype(BF16), seq=seq, tm=512, tn=512)
    kvf = norm_matmul(c_kv, kv_norm, zkv, zkv, wkv.astype(BF16), seq=seq, tm=512, tn=512, out_dtype=BF16)
    tables = _rope_tables(pos, MLA_ROPE, MLA_ROPE)
    nope_w = MLA_HEADS * MLA_NOPE
    q_rope = rope_cols(qf, tables, seq=seq, col0=nope_w // (MLA_HEADS * MLA_ROPE), n_heads=MLA_HEADS * MLA_ROPE // LANES,
                       half=MLA_ROPE // 2, tt=512)
    kr_pad = jnp.pad(k_r, ((0, 0), (0, LANES - MLA_ROPE)))
    k_rope = rope_cols(kr_pad, tables, seq=seq, col0=0, n_heads=1, half=MLA_ROPE // 2, tt=512)[:, :MLA_ROPE]
    H = MLA_HEADS
    q_nope = qf[:, :nope_w].astype(BF16).reshape(batch, seq, H, MLA_NOPE)
    q_cat = jnp.concatenate([q_nope, q_rope.reshape(batch, seq, H, MLA_ROPE)], -1)
    tq = 512
    q_cat = q_cat.transpose(0, 2, 1, 3).reshape(batch * H, seq // tq, tq, dqk)
    k_nope = kvf[:, :nope_w].reshape(batch, seq, H, MLA_NOPE)
    k_cat = jnp.concatenate([k_nope, jnp.broadcast_to(k_rope.reshape(batch, seq, 1, MLA_ROPE), (batch, seq, H, MLA_ROPE))], -1)
    k_cat = k_cat.transpose(0, 2, 1, 3).reshape(batch * H, seq, dqk)
    v = kvf[:, nope_w:].reshape(batch, seq, H, MLA_V).transpose(0, 2, 1, 3).reshape(batch * H, seq, MLA_V)
    o_b = flash_attn(q_cat, k_cat, v, qt=tq, tk=256, scale=1.0 / math.sqrt(dqk))
    o_b = o_b.reshape(batch, H, seq, MLA_V).transpose(0, 2, 1, 3).reshape(T, H * MLA_V)
    wo = w_out.astype(BF16)
    return matmul_residual([o_a, o_b], [wo[:sbw], wo[sbw:]], x, gate, seq=seq, tm=512, tn=min(512, D))


def nsa_layer(x, mod, pos, g_norm, w_in, pe_k, pe_v, w1_k, w2_k, w1_v, w2_v, w_out, *, batch, seq):
    T, D = x.shape
    shift, scale, gate = _split3(mod)
    G, R, d = NSA_GROUPS, NSA_HEADS // NSA_GROUPS, NSA_DIM
    H = NSA_HEADS
    kvw = G * d
    n_in = w_in.shape[1]
    n_pad = -(-n_in // 768) * 768
    p = norm_matmul(x, g_norm, scale, shift, _pad_cols(w_in, n_pad).astype(BF16), seq=seq, tm=512, tn=768)
    tables = _rope_tables(pos, ROT_DIM, d)
    half = ROT_DIM // 2
    qt = QBLK
    nQ = seq // qt
    q_st = rope_cols(p, tables, seq=seq, col0=0, n_heads=R, half=half, tt=qt, stacked_groups=G)
    q_st = q_st.reshape(batch, nQ, G, R * qt, d).transpose(0, 2, 1, 3, 4)
    base = H * d // kvw
    kc = rope_cols(p, tables, seq=seq, col0=base + 0, n_heads=G, half=half, tt=512)
    ks = rope_cols(p, tables, seq=seq, col0=base + 2, n_heads=G, half=half, tt=512)
    kw = rope_cols(p, tables, seq=seq, col0=base + 4, n_heads=G, half=half, tt=512)
    off = H * d
    vc = p[:, off + kvw:off + 2 * kvw].astype(BF16)
    vs = p[:, off + 3 * kvw:off + 4 * kvw].astype(BF16)
    vw = p[:, off + 5 * kvw:off + 6 * kvw].astype(BF16)
    gl = jnp.pad(p[:, off + 6 * kvw:n_in], ((0, 0), (0, LANES - 3 * H)))

    def per_group(a):
        return a.reshape(batch, seq, G, d).transpose(0, 2, 1, 3)

    M = seq // CMP_STRIDE
    n_cmp = (seq - CMP_BLK) // CMP_STRIDE + 1
    ncp = -(-M // LANES) * LANES
    def chunks(a):
        c = per_group(a).reshape(batch, G, M, CMP_STRIDE * d)
        return jnp.pad(c, ((0, 0), (0, 0), (0, ncp - M), (0, 0)))
    kc_c = nsa_compress(chunks(kc), pe_k, w1_k, w2_k)
    vc_c = nsa_compress(chunks(vc), pe_v, w1_v, w2_v)
    n_slc = seq // SLC_BLK
    slc_k = min(SLC_TOPK, n_slc)
    c_s = np.arange(ncp) * CMP_STRIDE
    s_s = np.arange(LANES) * SLC_BLK
    ovl = np.clip(np.minimum(c_s[:, None] + CMP_BLK, s_s[None, :] + SLC_BLK) - np.maximum(c_s[:, None], s_s[None, :]), 0, None)
    ovl[n_cmp:, :] = 0
    ovl[:, n_slc:] = 0
    scale_a = 1.0 / math.sqrt(d)
    oc, bias = nsa_cmp_select(q_st, kc_c, vc_c, jnp.asarray(ovl, BF16), qt=qt, n_cmp=n_cmp, n_slc=n_slc,
                              slc_k=slc_k, scale=scale_a)
    q_aug = jnp.concatenate([q_st, jnp.tile(bias, (1, 1, 1, R, 1))], -1).reshape(batch * G, nQ, R * qt, 2 * d)
    onehot = (np.arange(seq)[:, None] // SLC_BLK == np.arange(LANES)[None, :]).astype(np.float32)
    ks_g = per_group(ks)
    k_aug = jnp.concatenate([ks_g, jnp.broadcast_to(jnp.asarray(onehot, BF16), ks_g.shape[:2] + onehot.shape)], -1)
    k_aug = k_aug.reshape(batch * G, seq, 2 * d)
    osl = flash_attn(q_aug, k_aug, per_group(vs).reshape(batch * G, seq, d), qt=qt, tk=256, scale=scale_a)
    q_flat = q_st.reshape(batch * G, nQ, R * qt, d)
    ow = flash_attn(q_flat, per_group(kw).reshape(batch * G, seq, d), per_group(vw).reshape(batch * G, seq, d),
                    qt=qt, tk=128, scale=scale_a, window=WINDOW)

    def unstack(o):
        return o.reshape(batch, G, nQ, R, qt, d).transpose(0, 2, 1, 3, 4, 5).reshape(batch * nQ, G, R, qt, d)

    o = nsa_gate_combine(unstack(oc.reshape(batch * G, nQ, R * qt, d)), unstack(osl), unstack(ow), gl, R=R)
    return matmul_residual([o], [w_out.astype(BF16)], x, gate, seq=seq, tm=512, tn=min(512, D))


def peer_layer(x, mod, g_norm, w_q, k1, k2, u, v, *, batch, seq):
    shift, scale, gate = _split3(mod)
    q, h = norm_matmul(x, g_norm, scale, shift, w_q.astype(BF16), seq=seq, tm=512, tn=512, emit_h=True)
    experts, g = peer_route(q, k1, k2, heads=PEER_HEADS, topk=PEER_TOPK)
    table = peer_pack(u, v).reshape(u.shape[0], u.shape[1] // LANES, LANES)
    return peer_mix(experts, g, h, x, gate, table, seq=seq)


def kernel(x, c, positions, norm_mix, ada_mix_w, ada_mix_b, sbmla_w_in, mla_q_norm, mla_w_uq, mla_kv_norm, mla_w_ukv, sbmla_w_out, nsa_w_in, nsa_pe_k, nsa_pe_v, nsa_w1_k, nsa_w2_k, nsa_w1_v, nsa_w2_v, nsa_w_out, norm_ffn, ada_ffn_w, ada_ffn_b, peer_w_q, peer_k1, peer_k2, peer_u, peer_v, final_norm):
    B, S, D = x.shape
    depth = norm_mix.shape[0]
    mod_mix = modulation_all(c, ada_mix_w, ada_mix_b)
    mod_ffn = modulation_all(c, ada_ffn_w, ada_ffn_b)
    xs = x.reshape(B * S, D)
    for layer in range(depth):
        i = layer // 2
        if layer % 2 == 0:
            xs = sb_mla_layer(xs, mod_mix[layer], positions, norm_mix[layer], sbmla_w_in[i], mla_q_norm[i], mla_w_uq[i],
                              mla_kv_norm[i], mla_w_ukv[i], sbmla_w_out[i], batch=B, seq=S)
        else:
            xs = nsa_layer(xs, mod_mix[layer], positions, norm_mix[layer], nsa_w_in[i], nsa_pe_k[i], nsa_pe_v[i],
                           nsa_w1_k[i], nsa_w2_k[i], nsa_w1_v[i], nsa_w2_v[i], nsa_w_out[i], batch=B, seq=S)
        xs = peer_layer(xs, mod_ffn[layer], norm_ffn[layer], peer_w_q[layer], peer_k1[layer], peer_k2[layer],
                        peer_u[layer], peer_v[layer], batch=B, seq=S)
    return rmsnorm_rows(xs, final_norm).reshape(B, S, D)
```

```python
import functools
import math

import numpy as np
import jax
import jax.numpy as jnp
from jax import lax
from jax.experimental import pallas as pl
from jax.experimental.pallas import tpu as pltpu

F32 = jnp.float32
BF16 = jnp.bfloat16

QBLK = 128
ROPE_THETA = 500000.0
EPS = 1e-6
TINY = 1e-30
BIG = 1e9
NEG = -1e30

SB_HEADS = 8
SB_DIM = 128
MLA_HEADS = 8
MLA_Q_RANK = 512
MLA_KV_RANK = 256
MLA_NOPE = 128
MLA_ROPE = 64
MLA_V = 128
NSA_HEADS = 16
NSA_GROUPS = 2
NSA_DIM = 128
ROT_DIM = NSA_DIM // 4
CMP_BLK = 32
CMP_STRIDE = 16
CMP_HIDDEN = 256
SLC_BLK = 64
SLC_TOPK = 16
WINDOW = 512
PEER_HEADS = 8
PEER_KEYS = 128
PEER_DKEY = 256
PEER_TOPK = 16

LANES = 128
VMEM_LIMIT = 56 * 1024 * 1024


def _cparams(sem):
    return pltpu.CompilerParams(dimension_semantics=sem, vmem_limit_bytes=VMEM_LIMIT)


def _split(a):
    hi = a.astype(BF16)
    lo = (a - hi.astype(F32)).astype(BF16)
    return hi, lo


def _dot(a, b):
    return jnp.dot(a, b, preferred_element_type=F32)


def _dot_nt(a, b):
    return lax.dot_general(a, b, (((1,), (1,)), ((), ())), preferred_element_type=F32)


def _dot3(a, b):
    ah, al = _split(a)
    bh, bl = _split(b)
    return _dot(ah, bh) + (_dot(ah, bl) + _dot(al, bh))


def _dot3_nt(a, b):
    ah, al = _split(a)
    bh, bl = _split(b)
    return _dot_nt(ah, bh) + (_dot_nt(ah, bl) + _dot_nt(al, bh))


def _gelu_tanh(x):
    return 0.5 * x * (1.0 + jnp.tanh(math.sqrt(2.0 / math.pi) * (x + 0.044715 * (x * x * x))))


def _pad_cols(w, n):
    return jnp.pad(w, ((0, 0), (0, n - w.shape[1])))


def _mod_kernel(c_ref, w_ref, b_ref, o_ref):
    c = c_ref[...]
    s = c * jax.nn.sigmoid(c)
    o_ref[0] = _dot3(s, w_ref[0]) + b_ref[0]


def modulation_all(c, w, b):
    L, D, N = w.shape
    B = c.shape[0]
    rows = 8
    cp = jnp.pad(c, ((0, rows - B), (0, 0)))
    tn = 768 if N % 768 == 0 else N
    out = pl.pallas_call(
        _mod_kernel,
        grid=(L, N // tn),
        in_specs=[
            pl.BlockSpec((rows, D), lambda l, j: (0, 0)),
            pl.BlockSpec((1, D, tn), lambda l, j: (l, 0, j)),
            pl.BlockSpec((1, 1, tn), lambda l, j: (l, 0, j)),
        ],
        out_specs=pl.BlockSpec((1, rows, tn), lambda l, j: (l, 0, j)),
        out_shape=jax.ShapeDtypeStruct((L, rows, N), F32),
        compiler_params=_cparams(("arbitrary", "arbitrary")),
        name="adaln_mod",
    )(cp, w, b.reshape(L, 1, N))
    return out[:, :B]


def _norm_mm_kernel(x_ref, g_ref, sc_ref, sh_ref, w_ref, o_ref, *rest, emit_h):
    if emit_h:
        h_ref, hb_ref = rest
    else:
        (hb_ref,) = rest

    @pl.when(pl.program_id(1) == 0)
    def _():
        x = x_ref[...]
        y = x * lax.rsqrt(jnp.mean(x * x, axis=-1, keepdims=True) + EPS)
        h = (y * g_ref[...]) * (1.0 + sc_ref[...]) + sh_ref[...]
        hb_ref[...] = h.astype(BF16)
        if emit_h:
            h_ref[...] = h

    o_ref[...] = _dot(hb_ref[...], w_ref[...]).astype(o_ref.dtype)


def norm_matmul(x, g, scale, shift, w, *, seq, tm, tn, emit_h=False, out_dtype=F32):
    T, K = x.shape
    N = w.shape[1]
    assert T % tm == 0 and N % tn == 0 and seq % tm == 0
    nb = seq // tm
    B = scale.shape[0]
    out_shape = [jax.ShapeDtypeStruct((T, N), out_dtype)]
    out_specs = [pl.BlockSpec((tm, tn), lambda i, j: (i, j))]
    if emit_h:
        out_shape.append(jax.ShapeDtypeStruct((T, K), F32))
        out_specs.append(pl.BlockSpec((tm, K), lambda i, j: (i, 0)))
    res = pl.pallas_call(
        functools.partial(_norm_mm_kernel, emit_h=emit_h),
        grid=(T // tm, N // tn),
        in_specs=[
            pl.BlockSpec((tm, K), lambda i, j: (i, 0)),
            pl.BlockSpec((1, K), lambda i, j: (0, 0)),
            pl.BlockSpec((None, 1, K), lambda i, j: (i // nb, 0, 0)),
            pl.BlockSpec((None, 1, K), lambda i, j: (i // nb, 0, 0)),
            pl.BlockSpec((K, tn), lambda i, j: (0, j)),
        ],
        out_specs=out_specs,
        out_shape=out_shape,
        scratch_shapes=[pltpu.VMEM((tm, K), BF16)],
        compiler_params=_cparams(("arbitrary", "arbitrary")),
        name="norm_matmul",
    )(x, g.reshape(1, K), scale.reshape(B, 1, K), shift.reshape(B, 1, K), w)
    return res if emit_h else res[0]


def _mm_res_kernel(*refs, n_in):
    a_refs = refs[:n_in]
    w_refs = refs[n_in:2 * n_in]
    x_ref, gate_ref, o_ref = refs[2 * n_in:]
    y = _dot(a_refs[0][...], w_refs[0][...])
    for a_ref, w_ref in zip(a_refs[1:], w_refs[1:]):
        y = y + _dot(a_ref[...], w_ref[...])
    o_ref[...] = x_ref[...] + gate_ref[...] * y


def matmul_residual(a_list, w_list, x, gate, *, seq, tm, tn):
    T, D = x.shape
    B = gate.shape[0]
    nb = seq // tm
    n_in = len(a_list)
    in_specs = [pl.BlockSpec((tm, a.shape[1]), lambda i, j: (i, 0)) for a in a_list]
    in_specs += [pl.BlockSpec((w.shape[0], tn), lambda i, j: (0, j)) for w in w_list]
    in_specs += [
        pl.BlockSpec((tm, tn), lambda i, j: (i, j)),
        pl.BlockSpec((None, 1, tn), lambda i, j: (i // nb, 0, j)),
    ]
    return pl.pallas_call(
        functools.partial(_mm_res_kernel, n_in=n_in),
        grid=(T // tm, D // tn),
        in_specs=in_specs,
        out_specs=pl.BlockSpec((tm, tn), lambda i, j: (i, j)),
        out_shape=jax.ShapeDtypeStruct((T, D), F32),
        compiler_params=_cparams(("arbitrary", "arbitrary")),
        name="matmul_residual",
    )(*a_list, *w_list, x, gate.reshape(B, 1, D))


LOG2E = 1.4426950408889634


def _sb_kernel(q_ref, k_ref, v_ref, o_ref, c_ref, acc_ref, za_ref, zb_ref, qb_ref, *, tq, tk, scale):
    i = pl.program_id(1)
    qb_ref[...] = q_ref[...].astype(BF16)
    t_idx = i * tq + lax.broadcasted_iota(jnp.int32, (tq, 1), 0)
    col = lax.broadcasted_iota(jnp.int32, (1, tk), 1)
    tri = (lax.broadcasted_iota(jnp.int32, (tk, tk), 0) > lax.broadcasted_iota(jnp.int32, (tk, tk), 1)).astype(BF16)
    tri2 = jnp.concatenate([tri, tri], axis=0)
    c_ref[...] = jnp.zeros(c_ref.shape, F32)
    acc_ref[...] = jnp.zeros(acc_ref.shape, F32)
    top = 2 * i + 1

    def tile_start(n):
        return pl.multiple_of(jnp.maximum(top - n, 0) * tk, tk)

    def scores(n, z_ref):
        z_ref[...] = _dot_nt(qb_ref[...], k_ref[pl.ds(tile_start(n), tk), :].astype(BF16))

    def update(n, z_ref, masked):
        v = v_ref[pl.ds(tile_start(n), tk), :].astype(BF16)
        z = z_ref[...] * (scale * LOG2E)
        sp = jnp.maximum(z, 0.0) + jnp.log2(1.0 + jnp.exp2(-jnp.abs(z)))
        if masked:
            strict = ((top - n) * tk + col) < t_idx
            lp = jnp.where(strict, sp, 0.0)
        else:
            lp = sp
        hi, lo = _split(lp)
        suf = _dot(jnp.concatenate([hi, lo], axis=1), tri2)
        c = c_ref[...]
        a = jnp.exp2((z - sp) - (suf + c))
        if masked:
            a = jnp.where(strict, a, 0.0)
        acc_ref[...] += _dot(a.astype(BF16), v)
        c_ref[...] = c + jnp.sum(lp, axis=-1, keepdims=True)

    scores(0, za_ref)
    scores(1, zb_ref)
    update(0, za_ref, True)
    scores(2, za_ref)
    update(1, zb_ref, True)

    def body(m, carry):
        n = 2 * m
        scores(n + 1, zb_ref)
        update(n, za_ref, False)
        scores(n + 2, za_ref)
        update(n + 1, zb_ref, False)
        return carry

    lax.fori_loop(1, i + 1, body, 0)
    o_ref[...] = acc_ref[...].astype(o_ref.dtype)


def stick_breaking_attn(p, *, batch, seq, heads, dim, q_col, k_col, v_col, tq=512):
    nq = seq // tq
    tk = tq // 2
    return pl.pallas_call(
        functools.partial(_sb_kernel, tq=tq, tk=tk, scale=1.0 / math.sqrt(dim)),
        grid=(batch * heads, nq),
        in_specs=[
            pl.BlockSpec((tq, dim), lambda bh, i: ((bh // heads) * nq + i, q_col + bh % heads)),
            pl.BlockSpec((seq, dim), lambda bh, i: (bh // heads, k_col + bh % heads)),
            pl.BlockSpec((seq, dim), lambda bh, i: (bh // heads, v_col + bh % heads)),
        ],
        out_specs=pl.BlockSpec((tq, dim), lambda bh, i: ((bh // heads) * nq + i, bh % heads)),
        out_shape=jax.ShapeDtypeStruct((batch * seq, heads * dim), BF16),
        scratch_shapes=[pltpu.VMEM((tq, 1), F32), pltpu.VMEM((tq, dim), F32), pltpu.VMEM((tq, tk), F32),
                        pltpu.VMEM((tq, tk), F32), pltpu.VMEM((tq, dim), BF16)],
        compiler_params=_cparams(("arbitrary", "arbitrary")),
        name="stick_breaking",
    )(p, p, p)


FLASH_ROW_BLOCK = 128


def _flash_kernel(q_ref, k_ref, v_ref, o_ref, m_ref, acc_ref, sa_ref, sb_ref, p_ref, alpha_ref, *, qt, tk, scale, window):
    i = pl.program_id(1)
    rb = q_ref.shape[0]
    dv = v_ref.shape[1]
    nch = tk // LANES
    q0 = i * qt
    row = lax.broadcasted_iota(jnp.int32, (rb, 1), 0)
    tok = q0 + (row % qt if rb != qt else row)
    col = lax.broadcasted_iota(jnp.int32, (1, LANES), 1)
    hi_blk = (q0 + qt - 1) // tk
    lo_blk = 0 if window is None else jnp.maximum(q0 - window + 1, 0) // tk
    m_ref[...] = jnp.full(m_ref.shape, NEG, F32)
    acc_ref[...] = jnp.zeros(acc_ref.shape, F32)
    ones = jnp.ones((tk, LANES), BF16)

    def tile_start(j):
        return pl.multiple_of(jnp.minimum(j, hi_blk) * tk, tk)

    def scores(j, s_ref):
        s_ref[...] = _dot_nt(q_ref[...], k_ref[pl.ds(tile_start(j), tk), :])

    def update(j, s_ref, masked):
        v_aug = jnp.concatenate([v_ref[pl.ds(tile_start(j), tk), :], ones], axis=1)
        for r0 in range(0, rb, FLASH_ROW_BLOCK):
            rs = slice(r0, r0 + FLASH_ROW_BLOCK)
            chunks = [s_ref[rs, c * LANES:(c + 1) * LANES] * (scale * LOG2E) for c in range(nch)]
            if masked:
                masks = []
                for c in range(nch):
                    kpos = j * tk + c * LANES + col
                    mk = kpos <= tok[rs]
                    if window is not None:
                        mk = mk & (kpos > tok[rs] - window)
                    masks.append(mk)
                chunks = [jnp.where(mk, ch, NEG) for mk, ch in zip(masks, chunks)]
            cmax = chunks[0]
            for ch in chunks[1:]:
                cmax = jnp.maximum(cmax, ch)
            m_prev = m_ref[rs, :]
            m_new = jnp.maximum(m_prev, jnp.max(cmax, axis=-1, keepdims=True))
            ps = [jnp.exp2(ch - m_new) for ch in chunks]
            if masked:
                ps = [jnp.where(mk, p, 0.0) for mk, p in zip(masks, ps)]
            p_ref[rs, :] = jnp.concatenate(ps, axis=1).astype(BF16)
            alpha_ref[rs, :] = jnp.exp2(m_prev - m_new)
            m_ref[rs, :] = m_new
        pv = _dot(p_ref[...], v_aug)
        alpha = alpha_ref[...]
        for c in range((dv + LANES) // LANES):
            sl = slice(c * LANES, (c + 1) * LANES)
            acc_ref[:, sl] = alpha * acc_ref[:, sl] + pv[:, sl]

    scores(lo_blk, sa_ref)

    def pair(j, masked):
        scores(j + 1, sb_ref)
        update(j, sa_ref, masked)
        scores(j + 2, sa_ref)
        update(j + 1, sb_ref, masked)

    n_free = 0 if window is not None else ((q0 + 1) // tk) // 2

    def free_body(n, carry):
        pair(lo_blk + 2 * n, False)
        return carry

    def masked_body(n, carry):
        pair(lo_blk + 2 * n, True)
        return carry

    if window is None:
        lax.fori_loop(0, n_free, free_body, 0)
    lax.fori_loop(n_free, (hi_blk - lo_blk + 2) // 2, masked_body, 0)
    o_ref[...] = (acc_ref[:, :dv] / jnp.maximum(acc_ref[:, dv:], TINY)).astype(o_ref.dtype)


def flash_attn(q, k, v, *, qt, tk, scale, window=None, kv_rep=1, out_dtype=BF16):
    BH, nQ, rb, dq = q.shape
    S = k.shape[1]
    dv = v.shape[2]
    assert dv == LANES and tk % LANES == 0
    return pl.pallas_call(
        functools.partial(_flash_kernel, qt=qt, tk=tk, scale=scale, window=window),
        grid=(BH, nQ),
        in_specs=[
            pl.BlockSpec((None, None, rb, dq), lambda bh, i: (bh, i, 0, 0)),
            pl.BlockSpec((None, S, dq), lambda bh, i: (bh // kv_rep, 0, 0)),
            pl.BlockSpec((None, S, dv), lambda bh, i: (bh // kv_rep, 0, 0)),
        ],
        out_specs=pl.BlockSpec((None, None, rb, dv), lambda bh, i: (bh, i, 0, 0)),
        out_shape=jax.ShapeDtypeStruct((BH, nQ, rb, dv), out_dtype),
        scratch_shapes=[pltpu.VMEM((rb, LANES), F32), pltpu.VMEM((rb, dv + LANES), F32),
                        pltpu.VMEM((rb, tk), F32), pltpu.VMEM((rb, tk), F32),
                        pltpu.VMEM((rb, tk), BF16), pltpu.VMEM((rb, LANES), F32)],
        compiler_params=_cparams(("arbitrary", "arbitrary")),
        name="flash_attn",
    )(q, k, v)


def _rope_tables(pos, rot_dim, period):
    half = rot_dim // 2
    inv = ROPE_THETA ** (-jnp.arange(half, dtype=F32) / half)
    ang = pos.astype(F32)[..., None] * inv
    cos, sin = jnp.cos(ang), jnp.sin(ang)
    rest = period - rot_dim
    shp = cos.shape[:-1]
    c = jnp.concatenate([cos, cos, jnp.ones(shp + (rest,), F32)], -1)
    sn = jnp.concatenate([-sin, jnp.zeros(shp + (half + rest,), F32)], -1)
    sp = jnp.concatenate([jnp.zeros(shp + (half,), F32), sin, jnp.zeros(shp + (rest,), F32)], -1)
    rep = LANES // period
    return tuple(jnp.tile(t, (1, 1, rep)) for t in (c, sn, sp))


def _rope_lanes(x, c, sn, sp, half):
    return x * c + pltpu.roll(x, LANES - half, 1) * sn + pltpu.roll(x, half, 1) * sp


def _rope_kernel(x_ref, c_ref, sn_ref, sp_ref, o_ref, *, half, n_heads, stacked):
    c, sn, sp = c_ref[...], sn_ref[...], sp_ref[...]
    for h in range(n_heads):
        y = _rope_lanes(x_ref[:, h * LANES:(h + 1) * LANES], c, sn, sp, half).astype(o_ref.dtype)
        if stacked:
            o_ref[h] = y
        else:
            o_ref[:, h * LANES:(h + 1) * LANES] = y


def rope_cols(p, tables, *, seq, col0, n_heads, half, tt, stacked_groups=None):
    T = p.shape[0]
    nb = seq // tt
    width = n_heads * LANES
    tspec = pl.BlockSpec((None, tt, LANES), lambda i, g: (i // nb, i % nb, 0))
    if stacked_groups is None:
        grid = (T // tt, 1)
        out_spec = pl.BlockSpec((tt, width), lambda i, g: (i, 0))
        out_shape = jax.ShapeDtypeStruct((T, width), BF16)
    else:
        grid = (T // tt, stacked_groups)
        out_spec = pl.BlockSpec((None, None, n_heads, tt, LANES), lambda i, g: (i, g, 0, 0, 0))
        out_shape = jax.ShapeDtypeStruct((T // tt, stacked_groups, n_heads, tt, LANES), BF16)
    return pl.pallas_call(
        functools.partial(_rope_kernel, half=half, n_heads=n_heads, stacked=stacked_groups is not None),
        grid=grid,
        in_specs=[pl.BlockSpec((tt, width), lambda i, g: (i, col0 + g)), tspec, tspec, tspec],
        out_specs=out_spec,
        out_shape=out_shape,
        compiler_params=_cparams(("arbitrary", "arbitrary")),
        name="rope",
    )(p, *tables)


def _compress_kernel(t_ref, pe_ref, w1_ref, w2_ref, o_ref):
    t = t_ref[...]
    half = t.shape[1]
    w1a = w1_ref[:half, :]
    w1b = w1_ref[half:, :]
    pe = pe_ref[...]
    bias = _dot3(jnp.broadcast_to(pe[:, :half], (8, half)), w1a.astype(F32)) + _dot3(
        jnp.broadcast_to(pe[:, half:], (8, half)), w1b.astype(F32))
    a = _dot(t, w1a)
    b = _dot(t, w1b)
    n = a.shape[0]
    hid = a + pltpu.roll(b, n - 1, 0) + bias[0:1, :]
    o_ref[...] = _dot(_gelu_tanh(hid).astype(BF16), w2_ref[...]).astype(o_ref.dtype)


def nsa_compress(t2, pe, w1, w2):
    B, G, M, K = t2.shape
    d = w2.shape[1]
    return pl.pallas_call(
        _compress_kernel,
        grid=(B, G),
        in_specs=[
            pl.BlockSpec((None, None, M, K), lambda b, g: (b, g, 0, 0)),
            pl.BlockSpec((1, 2 * K), lambda b, g: (0, 0)),
            pl.BlockSpec((2 * K, w1.shape[1]), lambda b, g: (0, 0)),
            pl.BlockSpec(w2.shape, lambda b, g: (0, 0)),
        ],
        out_specs=pl.BlockSpec((None, None, M, d), lambda b, g: (b, g, 0, 0)),
        out_shape=jax.ShapeDtypeStruct((B, G, M, d), BF16),
        compiler_params=_cparams(("arbitrary", "arbitrary")),
        name="nsa_compress",
    )(t2, pe.reshape(1, 2 * K), w1.astype(BF16), w2.astype(BF16))


CMP_Q_BLOCKS = 4


def _cmp_select_kernel(q_ref, kc_ref, vc_ref, ov_ref, oc_ref, bias_ref, *, nblk, **kw):
    for b in range(nblk):
        _cmp_select_block(pl.program_id(2) * nblk + b, q_ref.at[b], kc_ref, vc_ref, ov_ref, oc_ref.at[b], bias_ref.at[b], **kw)


def _cmp_select_block(i, q_ref, kc_ref, vc_ref, ov_ref, oc_ref, bias_ref, *, qt, n_cmp, n_slc, slc_k, scale):
    rb = q_ref.shape[0]
    R = rb // qt
    ncp = kc_ref.shape[0]
    q = q_ref[...]
    tok = i * qt + lax.broadcasted_iota(jnp.int32, (rb, 1), 0) % qt
    n_id = lax.broadcasted_iota(jnp.int32, (1, ncp), 1)
    valid = ((n_id * CMP_STRIDE + (CMP_BLK - 1)) <= tok) & (n_id < n_cmp)
    s = _dot_nt(q, kc_ref[...]) * scale
    m = jnp.max(jnp.where(valid, s, NEG), axis=-1, keepdims=True)
    e = jnp.where(valid, jnp.exp(s - m), 0.0)
    p = e / jnp.maximum(jnp.sum(e, axis=-1, keepdims=True), TINY)
    oc_ref[...] = _dot(p.astype(BF16), vc_ref[...]).astype(oc_ref.dtype)
    psum = p[0:qt]
    for r in range(1, R):
        psum = psum + p[r * qt:(r + 1) * qt]
    hi, lo = _split(psum)
    ov = ov_ref[...]
    imp = _dot(hi, ov) + _dot(lo, ov)
    t1 = i * qt + lax.broadcasted_iota(jnp.int32, (qt, 1), 0)
    blk = lax.broadcasted_iota(jnp.int32, (1, LANES), 1)
    cur = t1 // SLC_BLK
    forced = (blk == 0) | (blk == cur) | (blk == cur - 1)
    ok = blk * SLC_BLK <= t1
    score = jnp.where(forced, BIG, jnp.where(ok, imp, -BIG))
    score = jnp.where(blk < n_slc, score, -jnp.inf)
    bias = jnp.full((qt, LANES), -BIG, F32)
    for _ in range(slc_k):
        mx = jnp.max(score, axis=-1, keepdims=True)
        idx = jnp.min(jnp.where(score == mx, blk, LANES), axis=-1, keepdims=True)
        hit = blk == idx
        bias = jnp.where(hit, 0.0, bias)
        score = jnp.where(hit, -jnp.inf, score)
    bias_ref[...] = bias.astype(bias_ref.dtype)


def nsa_cmp_select(q_st, kc_c, vc_c, overlap, *, qt, n_cmp, n_slc, slc_k, scale):
    B, G, nQ, rb, d = q_st.shape
    ncp = kc_c.shape[2]
    nblk = CMP_Q_BLOCKS if nQ % CMP_Q_BLOCKS == 0 else 1
    return pl.pallas_call(
        functools.partial(_cmp_select_kernel, nblk=nblk, qt=qt, n_cmp=n_cmp, n_slc=n_slc, slc_k=slc_k, scale=scale),
        grid=(B, G, nQ // nblk),
        in_specs=[
            pl.BlockSpec((None, None, nblk, rb, d), lambda b, g, i: (b, g, i, 0, 0)),
            pl.BlockSpec((None, None, ncp, d), lambda b, g, i: (b, g, 0, 0)),
            pl.BlockSpec((None, None, ncp, d), lambda b, g, i: (b, g, 0, 0)),
            pl.BlockSpec((ncp, LANES), lambda b, g, i: (0, 0)),
        ],
        out_specs=[
            pl.BlockSpec((None, None, nblk, rb, d), lambda b, g, i: (b, g, i, 0, 0)),
            pl.BlockSpec((None, None, nblk, qt, LANES), lambda b, g, i: (b, g, i, 0, 0)),
        ],
        out_shape=[
            jax.ShapeDtypeStruct((B, G, nQ, rb, d), BF16),
            jax.ShapeDtypeStruct((B, G, nQ, qt, LANES), BF16),
        ],
        compiler_params=_cparams(("arbitrary", "arbitrary", "arbitrary")),
        name="nsa_cmp_select",
    )(q_st, kc_c, vc_c, overlap)


def _gate_combine_kernel(oc_ref, os_ref, ow_ref, gl_ref, o_ref, *, R):
    gl = gl_ref[...]
    gs = jax.nn.sigmoid(gl)
    g = pl.program_id(1)
    for r in range(R):
        outs = []
        acc = None
        for br, ref in enumerate((oc_ref, os_ref, ow_ref)):
            lane = (g * R + r) * 3 + br
            sel = lax.broadcasted_iota(jnp.int32, gs.shape, 1) == lane
            gv = jnp.sum(jnp.where(sel, gs, 0.0), axis=-1, keepdims=True)
            term = gv * ref[r].astype(F32)
            acc = term if acc is None else acc + term
        o_ref[:, r * LANES:(r + 1) * LANES] = acc.astype(o_ref.dtype)


def nsa_gate_combine(oc, osl, ow, gl, *, R):
    nT, G, _, tt, d = oc.shape
    T = nT * tt
    ospec = pl.BlockSpec((None, None, R, tt, d), lambda i, g: (i, g, 0, 0, 0))
    return pl.pallas_call(
        functools.partial(_gate_combine_kernel, R=R),
        grid=(nT, G),
        in_specs=[ospec, ospec, ospec, pl.BlockSpec((tt, LANES), lambda i, g: (i, 0))],
        out_specs=pl.BlockSpec((tt, R * d), lambda i, g: (i, g)),
        out_shape=jax.ShapeDtypeStruct((T, G * R * d), BF16),
        compiler_params=_cparams(("arbitrary", "arbitrary")),
        name="nsa_gate_combine",
    )(oc, osl, ow, gl)


def _topk_rows(s, k):
    n = s.shape[0]
    iota = lax.broadcasted_iota(jnp.int32, s.shape, 0)
    vals, idxs = [], []
    for _ in range(k):
        m = jnp.max(s, axis=0, keepdims=True)
        idx = jnp.min(jnp.where(s == m, iota, n), axis=0, keepdims=True)
        vals.append(m)
        idxs.append(idx)
        s = jnp.where(iota == idx, -jnp.inf, s)
    return jnp.concatenate(vals, axis=0), jnp.concatenate(idxs, axis=0)


def _peer_route_kernel(q_ref, k1_ref, k2_ref, e_ref, g_ref, *, topk, n_keys):
    half = k1_ref.shape[1]
    s1 = _dot3_nt(k1_ref[...], q_ref[:, :half])
    s2 = _dot3_nt(k2_ref[...], q_ref[:, half:])
    v1, i1 = _topk_rows(s1, topk)
    v2, i2 = _topk_rows(s2, topk)
    assert topk == 2 * SUBLANES
    tb = s1.shape[1]
    sub = lax.broadcasted_iota(jnp.int32, (SUBLANES, tb), 0)
    cand, cidx, cpos = [], [], []
    for a in range(SUBLANES):
        for m in range(2 if a == 0 else 1):
            bs = slice(m * SUBLANES, (m + 1) * SUBLANES)
            cand.append(v1[a:a + 1, :] + v2[bs, :])
            cidx.append(i1[a:a + 1, :] * n_keys + i2[bs, :])
            cpos.append(a * topk + m * SUBLANES + sub)
    cand.append(v1[SUBLANES:, :] + v2[0:1, :])
    cidx.append(i1[SUBLANES:, :] * n_keys + i2[0:1, :])
    cpos.append((SUBLANES + sub) * topk)
    cand = jnp.concatenate(cand, axis=0)
    cidx = jnp.concatenate(cidx, axis=0)
    cpos = jnp.concatenate(cpos, axis=0)
    tops, exs = [], []
    for _ in range(topk):
        m = jnp.max(cand, axis=0, keepdims=True)
        p = jnp.min(jnp.where(cand == m, cpos, topk * topk), axis=0, keepdims=True)
        hit = cpos == p
        tops.append(m)
        exs.append(jnp.sum(jnp.where(hit, cidx, 0), axis=0, keepdims=True))
        cand = jnp.where(hit, -jnp.inf, cand)
    top = jnp.concatenate(tops, axis=0)
    w = jnp.exp(top - top[0:1])
    e_ref[...] = jnp.concatenate(exs, axis=0)
    g_ref[...] = w / jnp.sum(w, axis=0, keepdims=True)


def peer_route(q, k1, k2, *, heads, topk, tb=512):
    T, N = q.shape
    n_keys, half = k1.shape
    hk = heads * topk
    return pl.pallas_call(
        functools.partial(_peer_route_kernel, topk=topk, n_keys=n_keys),
        grid=(T // tb, heads),
        in_specs=[
            pl.BlockSpec((tb, 2 * half), lambda i, h: (i, h)),
            pl.BlockSpec((n_keys, half), lambda i, h: (0, 0)),
            pl.BlockSpec((n_keys, half), lambda i, h: (0, 0)),
        ],
        out_specs=[pl.BlockSpec((topk, tb), lambda i, h: (h, i)), pl.BlockSpec((topk, tb), lambda i, h: (h, i))],
        out_shape=[jax.ShapeDtypeStruct((hk, T), jnp.int32), jax.ShapeDtypeStruct((hk, T), F32)],
        compiler_params=_cparams(("arbitrary", "arbitrary")),
        name="peer_route",
    )(q, k1, k2)


def _pack_kernel(u_ref, v_ref, o_ref):
    ub = pltpu.bitcast(u_ref[...].astype(BF16).astype(F32), jnp.uint32) >> 16
    vb = pltpu.bitcast(v_ref[...].astype(BF16).astype(F32), jnp.uint32) & jnp.uint32(0xFFFF0000)
    o_ref[...] = vb | ub


def peer_pack(u, v, layer, te=512):
    _, E, D = u.shape
    in_spec = pl.BlockSpec((None, te, D), lambda i: (layer, i, 0))
    spec = pl.BlockSpec((te, D), lambda i: (i, 0))
    return pl.pallas_call(
        _pack_kernel,
        grid=(E // te,),
        in_specs=[in_spec, in_spec],
        out_specs=spec,
        out_shape=jax.ShapeDtypeStruct((E, D), jnp.uint32),
        compiler_params=_cparams(("arbitrary",)),
        name="peer_pack",
    )(u, v)


SUBLANES = 8


def _peer_mix_kernel(idx_ref, idxn_ref, h_ref, g_ref, x_ref, gate_ref, tbl_ref, o_ref, buf_ref, sem_ref, *, tb, nk):
    i = pl.program_id(0)
    n = pl.num_programs(0)
    rows = tb * nk
    nlt = buf_ref.shape[1]
    slot = i % 2
    hi_mask = jnp.uint32(0xFFFF0000)

    def row_copy(src_idx_ref, r, dst_slot):
        return pltpu.make_async_copy(tbl_ref.at[src_idx_ref[0, r]], buf_ref.at[dst_slot, :, r, :], sem_ref.at[dst_slot])

    def slot_copy(s):
        return pltpu.make_async_copy(buf_ref.at[s], buf_ref.at[s], sem_ref.at[s])

    @pl.when(i == 0)
    def _():
        def one(r, c):
            row_copy(idx_ref, r, 0).start()
            return c
        lax.fori_loop(0, rows, one, 0, unroll=8)

    slot_copy(slot).wait()

    ys = []
    for t in range(tb):
        hb = [jnp.broadcast_to(h_ref[t:t + 1, lt * LANES:(lt + 1) * LANES], (SUBLANES, LANES)) for lt in range(nlt)]
        yacc = [None] * nlt
        for c in range(nk // SUBLANES):
            r0 = t * nk + c * SUBLANES
            ws = [buf_ref[slot, lt, r0:r0 + SUBLANES, :] for lt in range(nlt)]
            s = None
            for lt in range(nlt):
                term = pltpu.bitcast(ws[lt] << 16, F32) * hb[lt]
                s = term if s is None else s + term
            act = jnp.sum(s, axis=-1, keepdims=True)
            a = g_ref[c * SUBLANES:(c + 1) * SUBLANES, t:t + 1] * _gelu_tanh(act)
            for lt in range(nlt):
                yv = a * pltpu.bitcast(ws[lt] & hi_mask, F32)
                yacc[lt] = yv if yacc[lt] is None else yacc[lt] + yv
            for r in range(r0, r0 + SUBLANES):
                row_copy(idxn_ref, r, 1 - slot).start(priority=r % 2)
        ys.append(jnp.concatenate([jnp.sum(ya, axis=0, keepdims=True) for ya in yacc], axis=1))
    o_ref[...] = x_ref[...] + gate_ref[...] * jnp.concatenate(ys, axis=0)

    @pl.when(i == n - 1)
    def _():
        slot_copy(1 - slot).wait()


def peer_mix(experts_t, g_t, h, x, gate, table, *, seq, tb=8):
    T, D = h.shape
    nk = experts_t.shape[0]
    B = gate.shape[0]
    nb = seq // tb
    nblk = T // tb
    idx = experts_t.T.reshape(nblk, 1, tb * nk)
    g = g_t.reshape(nk, nblk, tb).transpose(1, 0, 2)
    smem_spec = lambda f: pl.BlockSpec((None, 1, tb * nk), f, memory_space=pltpu.SMEM)
    return pl.pallas_call(
        functools.partial(_peer_mix_kernel, tb=tb, nk=nk),
        grid=(nblk,),
        in_specs=[
            smem_spec(lambda i: (i, 0, 0)),
            smem_spec(lambda i: (jnp.minimum(i + 1, nblk - 1), 0, 0)),
            pl.BlockSpec((tb, D), lambda i: (i, 0)),
            pl.BlockSpec((None, nk, tb), lambda i: (i, 0, 0)),
            pl.BlockSpec((tb, D), lambda i: (i, 0)),
            pl.BlockSpec((None, 1, D), lambda i: (i // nb, 0, 0)),
            pl.BlockSpec(memory_space=pl.ANY),
        ],
        out_specs=pl.BlockSpec((tb, D), lambda i: (i, 0)),
        out_shape=jax.ShapeDtypeStruct((T, D), F32),
        scratch_shapes=[pltpu.VMEM((2, D // LANES, tb * nk, LANES), jnp.uint32), pltpu.SemaphoreType.DMA((2,))],
        compiler_params=_cparams(("arbitrary",)),
        name="peer_mix",
    )(idx, idx, h, g, x, gate.reshape(B, 1, D), table)


def _rmsnorm_kernel(x_ref, g_ref, o_ref):
    x = x_ref[...]
    o_ref[...] = (x * lax.rsqrt(jnp.mean(x * x, axis=-1, keepdims=True) + EPS)) * g_ref[...]


def rmsnorm_rows(x, g, tm=512):
    T, D = x.shape
    return pl.pallas_call(
        _rmsnorm_kernel,
        grid=(T // tm,),
        in_specs=[pl.BlockSpec((tm, D), lambda i: (i, 0)), pl.BlockSpec((1, D), lambda i: (0, 0))],
        out_specs=pl.BlockSpec((tm, D), lambda i: (i, 0)),
        out_shape=jax.ShapeDtypeStruct((T, D), F32),
        compiler_params=_cparams(("arbitrary",)),
        name="final_rmsnorm",
    )(x, g.reshape(1, D))


def _split3(m):
    d = m.shape[-1] // 3
    return m[:, :d], m[:, d:2 * d], m[:, 2 * d:]


def sb_mla_layer(x, mod, pos, g_norm, w_in, q_norm, w_uq, kv_norm, w_ukv, w_out, *, batch, seq):
    T, D = x.shape
    shift, scale, gate = _split3(mod)
    sbw = SB_HEADS * SB_DIM
    n_in = w_in.shape[1]
    n_pad = -(-n_in // 512) * 512
    p = norm_matmul(x, g_norm, scale, shift, _pad_cols(w_in, n_pad).astype(BF16), seq=seq, tm=512, tn=512)
    o_a = stick_breaking_attn(p, batch=batch, seq=seq, heads=SB_HEADS, dim=SB_DIM,
                              q_col=0, k_col=SB_HEADS, v_col=2 * SB_HEADS)
    c_q = p[:, 3 * sbw:3 * sbw + MLA_Q_RANK]
    c_kv = p[:, 3 * sbw + MLA_Q_RANK:3 * sbw + MLA_Q_RANK + MLA_KV_RANK]
    k_r = p[:, 3 * sbw + MLA_Q_RANK + MLA_KV_RANK:n_in]
    zq = jnp.zeros((batch, MLA_Q_RANK), F32)
    zkv = jnp.zeros((batch, MLA_KV_RANK), F32)
    dqk = MLA_NOPE + MLA_ROPE
    wq = w_uq.reshape(MLA_Q_RANK, MLA_HEADS, dqk)
    wq = jnp.concatenate([wq[:, :, :MLA_NOPE].reshape(MLA_Q_RANK, -1), wq[:, :, MLA_NOPE:].reshape(MLA_Q_RANK, -1)], 1)
    wkv = w_ukv.reshape(MLA_KV_RANK, MLA_HEADS, MLA_NOPE + MLA_V)
    wkv = jnp.concatenate([wkv[:, :, :MLA_NOPE].reshape(MLA_KV_RANK, -1), wkv[:, :, MLA_NOPE:].reshape(MLA_KV_RANK, -1)], 1)
    qf = norm_matmul(c_q, q_norm, zq, zq, wq.astype(BF16), seq=seq, tm=512, tn=512)
    kvf = norm_matmul(c_kv, kv_norm, zkv, zkv, wkv.astype(BF16), seq=seq, tm=512, tn=512, out_dtype=BF16)
    tables = _rope_tables(pos, MLA_ROPE, MLA_ROPE)
    nope_w = MLA_HEADS * MLA_NOPE
    q_rope = rope_cols(qf, tables, seq=seq, col0=nope_w // (MLA_HEADS * MLA_ROPE), n_heads=MLA_HEADS * MLA_ROPE // LANES,
                       half=MLA_ROPE // 2, tt=512)
    kr_pad = jnp.pad(k_r, ((0, 0), (0, LANES - MLA_ROPE)))
    k_rope = rope_cols(kr_pad, tables, seq=seq, col0=0, n_heads=1, half=MLA_ROPE // 2, tt=512)[:, :MLA_ROPE]
    H = MLA_HEADS
    q_nope = qf[:, :nope_w].astype(BF16).reshape(batch, seq, H, MLA_NOPE)
    q_cat = jnp.concatenate([q_nope, q_rope.reshape(batch, seq, H, MLA_ROPE)], -1)
    tq = 512
    q_cat = q_cat.transpose(0, 2, 1, 3).reshape(batch * H, seq // tq, tq, dqk)
    k_nope = kvf[:, :nope_w].reshape(batch, seq, H, MLA_NOPE)
    k_cat = jnp.concatenate([k_nope, jnp.broadcast_to(k_rope.reshape(batch, seq, 1, MLA_ROPE), (batch, seq, H, MLA_ROPE))], -1)
    k_cat = k_cat.transpose(0, 2, 1, 3).reshape(batch * H, seq, dqk)
    v = kvf[:, nope_w:].reshape(batch, seq, H, MLA_V).transpose(0, 2, 1, 3).reshape(batch * H, seq, MLA_V)
    o_b = flash_attn(q_cat, k_cat, v, qt=tq, tk=256, scale=1.0 / math.sqrt(dqk))
    o_b = o_b.reshape(batch, H, seq, MLA_V).transpose(0, 2, 1, 3).reshape(T, H * MLA_V)
    wo = w_out.astype(BF16)
    return matmul_residual([o_a, o_b], [wo[:sbw], wo[sbw:]], x, gate, seq=seq, tm=512, tn=min(512, D))


def nsa_layer(x, mod, pos, g_norm, w_in, pe_k, pe_v, w1_k, w2_k, w1_v, w2_v, w_out, *, batch, seq):
    T, D = x.shape
    shift, scale, gate = _split3(mod)
    G, R, d = NSA_GROUPS, NSA_HEADS // NSA_GROUPS, NSA_DIM
    H = NSA_HEADS
    kvw = G * d
    n_in = w_in.shape[1]
    n_pad = -(-n_in // 768) * 768
    p = norm_matmul(x, g_norm, scale, shift, _pad_cols(w_in, n_pad).astype(BF16), seq=seq, tm=512, tn=768)
    tables = _rope_tables(pos, ROT_DIM, d)
    half = ROT_DIM // 2
    qt = QBLK
    nQ = seq // qt
    q_st = rope_cols(p, tables, seq=seq, col0=0, n_heads=R, half=half, tt=qt, stacked_groups=G)
    q_st = q_st.reshape(batch, nQ, G, R * qt, d).transpose(0, 2, 1, 3, 4)
    base = H * d // kvw
    kc = rope_cols(p, tables, seq=seq, col0=base + 0, n_heads=G, half=half, tt=512)
    ks = rope_cols(p, tables, seq=seq, col0=base + 2, n_heads=G, half=half, tt=512)
    kw = rope_cols(p, tables, seq=seq, col0=base + 4, n_heads=G, half=half, tt=512)
    off = H * d
    vc = p[:, off + kvw:off + 2 * kvw].astype(BF16)
    vs = p[:, off + 3 * kvw:off + 4 * kvw].astype(BF16)
    vw = p[:, off + 5 * kvw:off + 6 * kvw].astype(BF16)
    gl = jnp.pad(p[:, off + 6 * kvw:n_in], ((0, 0), (0, LANES - 3 * H)))

    def per_group(a):
        return a.reshape(batch, seq, G, d).transpose(0, 2, 1, 3)

    M = seq // CMP_STRIDE
    n_cmp = (seq - CMP_BLK) // CMP_STRIDE + 1
    ncp = -(-M // LANES) * LANES
    def chunks(a):
        c = per_group(a).reshape(batch, G, M, CMP_STRIDE * d)
        return jnp.pad(c, ((0, 0), (0, 0), (0, ncp - M), (0, 0)))
    kc_c = nsa_compress(chunks(kc), pe_k, w1_k, w2_k)
    vc_c = nsa_compress(chunks(vc), pe_v, w1_v, w2_v)
    n_slc = seq // SLC_BLK
    slc_k = min(SLC_TOPK, n_slc)
    c_s = np.arange(ncp) * CMP_STRIDE
    s_s = np.arange(LANES) * SLC_BLK
    ovl = np.clip(np.minimum(c_s[:, None] + CMP_BLK, s_s[None, :] + SLC_BLK) - np.maximum(c_s[:, None], s_s[None, :]), 0, None)
    ovl[n_cmp:, :] = 0
    ovl[:, n_slc:] = 0
    scale_a = 1.0 / math.sqrt(d)
    oc, bias = nsa_cmp_select(q_st, kc_c, vc_c, jnp.asarray(ovl, BF16), qt=qt, n_cmp=n_cmp, n_slc=n_slc,
                              slc_k=slc_k, scale=scale_a)
    q_aug = jnp.concatenate([q_st, jnp.tile(bias, (1, 1, 1, R, 1))], -1).reshape(batch * G, nQ, R * qt, 2 * d)
    onehot = (np.arange(seq)[:, None] // SLC_BLK == np.arange(LANES)[None, :]).astype(np.float32)
    ks_g = per_group(ks)
    k_aug = jnp.concatenate([ks_g, jnp.broadcast_to(jnp.asarray(onehot, BF16), ks_g.shape[:2] + onehot.shape)], -1)
    k_aug = k_aug.reshape(batch * G, seq, 2 * d)
    osl = flash_attn(q_aug, k_aug, per_group(vs).reshape(batch * G, seq, d), qt=qt, tk=256, scale=scale_a)
    q_flat = q_st.reshape(batch * G, nQ, R * qt, d)
    ow = flash_attn(q_flat, per_group(kw).reshape(batch * G, seq, d), per_group(vw).reshape(batch * G, seq, d),
                    qt=qt, tk=128, scale=scale_a, window=WINDOW)

    def unstack(o):
        return o.reshape(batch, G, nQ, R, qt, d).transpose(0, 2, 1, 3, 4, 5).reshape(batch * nQ, G, R, qt, d)

    o = nsa_gate_combine(unstack(oc.reshape(batch * G, nQ, R * qt, d)), unstack(osl), unstack(ow), gl, R=R)
    return matmul_residual([o], [w_out.astype(BF16)], x, gate, seq=seq, tm=512, tn=min(512, D))


def peer_layer(x, mod, g_norm, w_q, k1, k2, u_all, v_all, layer, *, batch, seq):
    shift, scale, gate = _split3(mod)
    q, h = norm_matmul(x, g_norm, scale, shift, w_q.astype(BF16), seq=seq, tm=512, tn=512, emit_h=True)
    experts, g = peer_route(q, k1, k2, heads=PEER_HEADS, topk=PEER_TOPK)
    table = peer_pack(u_all, v_all, layer).reshape(u_all.shape[1], u_all.shape[2] // LANES, LANES)
    return peer_mix(experts, g, h, x, gate, table, seq=seq)


def kernel(x, c, positions, norm_mix, ada_mix_w, ada_mix_b, sbmla_w_in, mla_q_norm, mla_w_uq, mla_kv_norm, mla_w_ukv, sbmla_w_out, nsa_w_in, nsa_pe_k, nsa_pe_v, nsa_w1_k, nsa_w2_k, nsa_w1_v, nsa_w2_v, nsa_w_out, norm_ffn, ada_ffn_w, ada_ffn_b, peer_w_q, peer_k1, peer_k2, peer_u, peer_v, final_norm):
    B, S, D = x.shape
    depth = norm_mix.shape[0]
    mod_mix = modulation_all(c, ada_mix_w, ada_mix_b)
    mod_ffn = modulation_all(c, ada_ffn_w, ada_ffn_b)
    xs = x.reshape(B * S, D)
    for layer in range(depth):
        i = layer // 2
        if layer % 2 == 0:
            xs = sb_mla_layer(xs, mod_mix[layer], positions, norm_mix[layer], sbmla_w_in[i], mla_q_norm[i], mla_w_uq[i],
                              mla_kv_norm[i], mla_w_ukv[i], sbmla_w_out[i], batch=B, seq=S)
        else:
            xs = nsa_layer(xs, mod_mix[layer], positions, norm_mix[layer], nsa_w_in[i], nsa_pe_k[i], nsa_pe_v[i],
                           nsa_w1_k[i], nsa_w2_k[i], nsa_w1_v[i], nsa_w2_v[i], nsa_w_out[i], batch=B, seq=S)
        xs = peer_layer(xs, mod_ffn[layer], norm_ffn[layer], peer_w_q[layer], peer_k1[layer], peer_k2[layer],
                        peer_u, peer_v, layer, batch=B, seq=S)
    return rmsnorm_rows(xs, final_norm).reshape(B, S, D)
```

```python
import functools
import math

import numpy as np
import jax
import jax.numpy as jnp
from jax import lax
from jax.experimental import pallas as pl
from jax.experimental.pallas import tpu as pltpu

F32 = jnp.float32
BF16 = jnp.bfloat16

QBLK = 128
ROPE_THETA = 500000.0
EPS = 1e-6
TINY = 1e-30
BIG = 1e9
NEG = -1e30

SB_HEADS = 8
SB_DIM = 128
MLA_HEADS = 8
MLA_Q_RANK = 512
MLA_KV_RANK = 256
MLA_NOPE = 128
MLA_ROPE = 64
MLA_V = 128
NSA_HEADS = 16
NSA_GROUPS = 2
NSA_DIM = 128
ROT_DIM = NSA_DIM // 4
CMP_BLK = 32
CMP_STRIDE = 16
CMP_HIDDEN = 256
SLC_BLK = 64
SLC_TOPK = 16
WINDOW = 512
PEER_HEADS = 8
PEER_KEYS = 128
PEER_DKEY = 256
PEER_TOPK = 16

LANES = 128
SUBLANES = 8
VMEM_LIMIT = 56 * 1024 * 1024


def _cparams(sem):
    return pltpu.CompilerParams(dimension_semantics=sem, vmem_limit_bytes=VMEM_LIMIT)


def _split(a):
    hi = a.astype(BF16)
    lo = (a - hi.astype(F32)).astype(BF16)
    return hi, lo


def _dot(a, b):
    return jnp.dot(a, b, preferred_element_type=F32)


def _dot_nt(a, b):
    return lax.dot_general(a, b, (((1,), (1,)), ((), ())), preferred_element_type=F32)


def _dot3(a, b):
    ah, al = _split(a)
    bh, bl = _split(b)
    return _dot(ah, bh) + (_dot(ah, bl) + _dot(al, bh))


def _dot3_nt(a, b):
    ah, al = _split(a)
    bh, bl = _split(b)
    return _dot_nt(ah, bh) + (_dot_nt(ah, bl) + _dot_nt(al, bh))


def _gelu_tanh(x):
    return 0.5 * x * (1.0 + jnp.tanh(math.sqrt(2.0 / math.pi) * (x + 0.044715 * (x * x * x))))


def _pad_cols(w, n):
    return jnp.pad(w, ((0, 0), (0, n - w.shape[1])))


def _mod_kernel(c_ref, w_ref, b_ref, o_ref):
    c = c_ref[...]
    s = c * jax.nn.sigmoid(c)
    o_ref[0] = _dot3(s, w_ref[0]) + b_ref[0]


def modulation_all(c, w, b):
    L, D, N = w.shape
    B = c.shape[0]
    rows = 8
    cp = jnp.pad(c, ((0, rows - B), (0, 0)))
    tn = 768 if N % 768 == 0 else N
    out = pl.pallas_call(
        _mod_kernel,
        grid=(L, N // tn),
        in_specs=[
            pl.BlockSpec((rows, D), lambda l, j: (0, 0)),
            pl.BlockSpec((1, D, tn), lambda l, j: (l, 0, j)),
            pl.BlockSpec((1, 1, tn), lambda l, j: (l, 0, j)),
        ],
        out_specs=pl.BlockSpec((1, rows, tn), lambda l, j: (l, 0, j)),
        out_shape=jax.ShapeDtypeStruct((L, rows, N), F32),
        compiler_params=_cparams(("arbitrary", "arbitrary")),
        name="adaln_mod",
    )(cp, w, b.reshape(L, 1, N))
    return out[:, :B]


def _norm_mm_kernel(x_ref, g_ref, sc_ref, sh_ref, w_ref, o_ref, *rest, emit_h):
    if emit_h:
        h_ref, hb_ref = rest
    else:
        (hb_ref,) = rest

    @pl.when(pl.program_id(1) == 0)
    def _():
        x = x_ref[...]
        y = x * lax.rsqrt(jnp.mean(x * x, axis=-1, keepdims=True) + EPS)
        h = (y * g_ref[...]) * (1.0 + sc_ref[...]) + sh_ref[...]
        hb_ref[...] = h.astype(BF16)
        if emit_h:
            h_ref[...] = h

    o_ref[...] = _dot(hb_ref[...], w_ref[...]).astype(o_ref.dtype)


def norm_matmul(x, g, scale, shift, w, *, seq, tm, tn, emit_h=False, out_dtype=F32):
    T, K = x.shape
    N = w.shape[1]
    assert T % tm == 0 and N % tn == 0 and seq % tm == 0
    nb = seq // tm
    B = scale.shape[0]
    out_shape = [jax.ShapeDtypeStruct((T, N), out_dtype)]
    out_specs = [pl.BlockSpec((tm, tn), lambda i, j: (i, j))]
    if emit_h:
        out_shape.append(jax.ShapeDtypeStruct((T, K), F32))
        out_specs.append(pl.BlockSpec((tm, K), lambda i, j: (i, 0)))
    res = pl.pallas_call(
        functools.partial(_norm_mm_kernel, emit_h=emit_h),
        grid=(T // tm, N // tn),
        in_specs=[
            pl.BlockSpec((tm, K), lambda i, j: (i, 0)),
            pl.BlockSpec((1, K), lambda i, j: (0, 0)),
            pl.BlockSpec((None, 1, K), lambda i, j: (i // nb, 0, 0)),
            pl.BlockSpec((None, 1, K), lambda i, j: (i // nb, 0, 0)),
            pl.BlockSpec((K, tn), lambda i, j: (0, j)),
        ],
        out_specs=out_specs,
        out_shape=out_shape,
        scratch_shapes=[pltpu.VMEM((tm, K), BF16)],
        compiler_params=_cparams(("arbitrary", "arbitrary")),
        name="norm_matmul",
    )(x, g.reshape(1, K), scale.reshape(B, 1, K), shift.reshape(B, 1, K), w)
    return res if emit_h else res[0]


def _mm_res_kernel(*refs, n_in):
    a_refs = refs[:n_in]
    w_refs = refs[n_in:2 * n_in]
    x_ref, gate_ref, o_ref = refs[2 * n_in:]
    y = _dot(a_refs[0][...], w_refs[0][...])
    for a_ref, w_ref in zip(a_refs[1:], w_refs[1:]):
        y = y + _dot(a_ref[...], w_ref[...])
    o_ref[...] = x_ref[...] + gate_ref[...] * y


def matmul_residual(a_list, w_list, x, gate, *, seq, tm, tn):
    T, D = x.shape
    B = gate.shape[0]
    nb = seq // tm
    n_in = len(a_list)
    in_specs = [pl.BlockSpec((tm, a.shape[1]), lambda i, j: (i, 0)) for a in a_list]
    in_specs += [pl.BlockSpec((w.shape[0], tn), lambda i, j: (0, j)) for w in w_list]
    in_specs += [
        pl.BlockSpec((tm, tn), lambda i, j: (i, j)),
        pl.BlockSpec((None, 1, tn), lambda i, j: (i // nb, 0, j)),
    ]
    return pl.pallas_call(
        functools.partial(_mm_res_kernel, n_in=n_in),
        grid=(T // tm, D // tn),
        in_specs=in_specs,
        out_specs=pl.BlockSpec((tm, tn), lambda i, j: (i, j)),
        out_shape=jax.ShapeDtypeStruct((T, D), F32),
        compiler_params=_cparams(("arbitrary", "arbitrary")),
        name="matmul_residual",
    )(*a_list, *w_list, x, gate.reshape(B, 1, D))


LOG2E = 1.4426950408889634


def _sb_kernel(q_ref, k_ref, v_ref, o_ref, c_ref, acc_ref, za_ref, zb_ref, qb_ref, *, tq, tk, scale):
    i = pl.program_id(1)
    qb_ref[...] = q_ref[...].astype(BF16)
    t_idx = i * tq + lax.broadcasted_iota(jnp.int32, (tq, 1), 0)
    col = lax.broadcasted_iota(jnp.int32, (1, tk), 1)
    tri = (lax.broadcasted_iota(jnp.int32, (tk, tk), 0) > lax.broadcasted_iota(jnp.int32, (tk, tk), 1)).astype(BF16)
    tri2 = jnp.concatenate([tri, tri], axis=0)
    c_ref[...] = jnp.zeros(c_ref.shape, F32)
    acc_ref[...] = jnp.zeros(acc_ref.shape, F32)
    top = 2 * i + 1

    def tile_start(n):
        return pl.multiple_of(jnp.maximum(top - n, 0) * tk, tk)

    def scores(n, z_ref):
        z_ref[...] = _dot_nt(qb_ref[...], k_ref[pl.ds(tile_start(n), tk), :].astype(BF16))

    def update(n, z_ref, masked):
        v = v_ref[pl.ds(tile_start(n), tk), :].astype(BF16)
        z = z_ref[...] * (scale * LOG2E)
        sp = jnp.maximum(z, 0.0) + jnp.log2(1.0 + jnp.exp2(-jnp.abs(z)))
        if masked:
            strict = ((top - n) * tk + col) < t_idx
            lp = jnp.where(strict, sp, 0.0)
        else:
            lp = sp
        hi, lo = _split(lp)
        suf = _dot(jnp.concatenate([hi, lo], axis=1), tri2)
        c = c_ref[...]
        a = jnp.exp2((z - sp) - (suf + c))
        if masked:
            a = jnp.where(strict, a, 0.0)
        acc_ref[...] += _dot(a.astype(BF16), v)
        c_ref[...] = c + jnp.sum(lp, axis=-1, keepdims=True)

    scores(0, za_ref)
    scores(1, zb_ref)
    update(0, za_ref, True)
    scores(2, za_ref)
    update(1, zb_ref, True)

    def body(m, carry):
        n = 2 * m
        scores(n + 1, zb_ref)
        update(n, za_ref, False)
        scores(n + 2, za_ref)
        update(n + 1, zb_ref, False)
        return carry

    lax.fori_loop(1, i + 1, body, 0)
    o_ref[...] = acc_ref[...].astype(o_ref.dtype)


def stick_breaking_attn(p, *, batch, seq, heads, dim, q_col, k_col, v_col, tq=512):
    nq = seq // tq
    tk = tq // 2
    return pl.pallas_call(
        functools.partial(_sb_kernel, tq=tq, tk=tk, scale=1.0 / math.sqrt(dim)),
        grid=(batch * heads, nq),
        in_specs=[
            pl.BlockSpec((tq, dim), lambda bh, i: ((bh // heads) * nq + i, q_col + bh % heads)),
            pl.BlockSpec((seq, dim), lambda bh, i: (bh // heads, k_col + bh % heads)),
            pl.BlockSpec((seq, dim), lambda bh, i: (bh // heads, v_col + bh % heads)),
        ],
        out_specs=pl.BlockSpec((tq, dim), lambda bh, i: ((bh // heads) * nq + i, bh % heads)),
        out_shape=jax.ShapeDtypeStruct((batch * seq, heads * dim), BF16),
        scratch_shapes=[pltpu.VMEM((tq, 1), F32), pltpu.VMEM((tq, dim), F32), pltpu.VMEM((tq, tk), F32),
                        pltpu.VMEM((tq, tk), F32), pltpu.VMEM((tq, dim), BF16)],
        compiler_params=_cparams(("arbitrary", "arbitrary")),
        name="stick_breaking",
    )(p, p, p)


FLASH_ROW_BLOCK = 128


def _flash_kernel(q_ref, k_ref, v_ref, o_ref, m_ref, acc_ref, sa_ref, sb_ref, p_ref, alpha_ref, *, qt, tk, scale, window):
    i = pl.program_id(1)
    rb = q_ref.shape[0]
    dv = v_ref.shape[1]
    nch = tk // LANES
    q0 = i * qt
    row = lax.broadcasted_iota(jnp.int32, (rb, 1), 0)
    tok = q0 + (row % qt if rb != qt else row)
    col = lax.broadcasted_iota(jnp.int32, (1, LANES), 1)
    hi_blk = (q0 + qt - 1) // tk
    lo_blk = 0 if window is None else jnp.maximum(q0 - window + 1, 0) // tk
    m_ref[...] = jnp.full(m_ref.shape, NEG, F32)
    acc_ref[...] = jnp.zeros(acc_ref.shape, F32)
    ones = jnp.ones((tk, LANES), BF16)

    def tile_start(j):
        return pl.multiple_of(jnp.minimum(j, hi_blk) * tk, tk)

    def scores(j, s_ref):
        s_ref[...] = _dot_nt(q_ref[...], k_ref[pl.ds(tile_start(j), tk), :])

    def update(j, s_ref, masked):
        v_aug = jnp.concatenate([v_ref[pl.ds(tile_start(j), tk), :], ones], axis=1)
        for r0 in range(0, rb, FLASH_ROW_BLOCK):
            rs = slice(r0, r0 + FLASH_ROW_BLOCK)
            chunks = [s_ref[rs, c * LANES:(c + 1) * LANES] * (scale * LOG2E) for c in range(nch)]
            if masked:
                masks = []
                for c in range(nch):
                    kpos = j * tk + c * LANES + col
                    mk = kpos <= tok[rs]
                    if window is not None:
                        mk = mk & (kpos > tok[rs] - window)
                    masks.append(mk)
                chunks = [jnp.where(mk, ch, NEG) for mk, ch in zip(masks, chunks)]
            cmax = chunks[0]
            for ch in chunks[1:]:
                cmax = jnp.maximum(cmax, ch)
            m_prev = m_ref[rs, :]
            m_new = jnp.maximum(m_prev, jnp.max(cmax, axis=-1, keepdims=True))
            ps = [jnp.exp2(ch - m_new) for ch in chunks]
            if masked:
                ps = [jnp.where(mk, p, 0.0) for mk, p in zip(masks, ps)]
            p_ref[rs, :] = jnp.concatenate(ps, axis=1).astype(BF16)
            alpha_ref[rs, :] = jnp.exp2(m_prev - m_new)
            m_ref[rs, :] = m_new
        pv = _dot(p_ref[...], v_aug)
        alpha = alpha_ref[...]
        for c in range((dv + LANES) // LANES):
            sl = slice(c * LANES, (c + 1) * LANES)
            acc_ref[:, sl] = alpha * acc_ref[:, sl] + pv[:, sl]

    scores(lo_blk, sa_ref)

    def pair(j, masked):
        scores(j + 1, sb_ref)
        update(j, sa_ref, masked)
        scores(j + 2, sa_ref)
        update(j + 1, sb_ref, masked)

    n_free = 0 if window is not None else ((q0 + 1) // tk) // 2

    def free_body(n, carry):
        pair(lo_blk + 2 * n, False)
        return carry

    def masked_body(n, carry):
        pair(lo_blk + 2 * n, True)
        return carry

    if window is None:
        lax.fori_loop(0, n_free, free_body, 0)
    lax.fori_loop(n_free, (hi_blk - lo_blk + 2) // 2, masked_body, 0)
    o_ref[...] = (acc_ref[:, :dv] / jnp.maximum(acc_ref[:, dv:], TINY)).astype(o_ref.dtype)


def flash_attn(q, k, v, *, qt, tk, scale, window=None, kv_rep=1, out_dtype=BF16):
    BH, nQ, rb, dq = q.shape
    S = k.shape[1]
    dv = v.shape[2]
    assert dv == LANES and tk % LANES == 0
    return pl.pallas_call(
        functools.partial(_flash_kernel, qt=qt, tk=tk, scale=scale, window=window),
        grid=(BH, nQ),
        in_specs=[
            pl.BlockSpec((None, None, rb, dq), lambda bh, i: (bh, i, 0, 0)),
            pl.BlockSpec((None, S, dq), lambda bh, i: (bh // kv_rep, 0, 0)),
            pl.BlockSpec((None, S, dv), lambda bh, i: (bh // kv_rep, 0, 0)),
        ],
        out_specs=pl.BlockSpec((None, None, rb, dv), lambda bh, i: (bh, i, 0, 0)),
        out_shape=jax.ShapeDtypeStruct((BH, nQ, rb, dv), out_dtype),
        scratch_shapes=[pltpu.VMEM((rb, LANES), F32), pltpu.VMEM((rb, dv + LANES), F32),
                        pltpu.VMEM((rb, tk), F32), pltpu.VMEM((rb, tk), F32),
                        pltpu.VMEM((rb, tk), BF16), pltpu.VMEM((rb, LANES), F32)],
        compiler_params=_cparams(("arbitrary", "arbitrary")),
        name="flash_attn",
    )(q, k, v)


def _rope_tables(pos, rot_dim, period):
    half = rot_dim // 2
    inv = ROPE_THETA ** (-jnp.arange(half, dtype=F32) / half)
    ang = pos.astype(F32)[..., None] * inv
    cos, sin = jnp.cos(ang), jnp.sin(ang)
    rest = period - rot_dim
    shp = cos.shape[:-1]
    c = jnp.concatenate([cos, cos, jnp.ones(shp + (rest,), F32)], -1)
    sn = jnp.concatenate([-sin, jnp.zeros(shp + (half + rest,), F32)], -1)
    sp = jnp.concatenate([jnp.zeros(shp + (half,), F32), sin, jnp.zeros(shp + (rest,), F32)], -1)
    rep = LANES // period
    return tuple(jnp.tile(t, (1, 1, rep)) for t in (c, sn, sp))


def _rope_lanes(x, c, sn, sp, half):
    return x * c + pltpu.roll(x, LANES - half, 1) * sn + pltpu.roll(x, half, 1) * sp


def _rope_kernel(x_ref, c_ref, sn_ref, sp_ref, o_ref, *, half, n_heads, stacked):
    c, sn, sp = c_ref[...], sn_ref[...], sp_ref[...]
    for h in range(n_heads):
        y = _rope_lanes(x_ref[:, h * LANES:(h + 1) * LANES], c, sn, sp, half).astype(o_ref.dtype)
        if stacked:
            o_ref[h] = y
        else:
            o_ref[:, h * LANES:(h + 1) * LANES] = y


def rope_cols(p, tables, *, seq, col0, n_heads, half, tt, stacked_groups=None):
    T = p.shape[0]
    nb = seq // tt
    width = n_heads * LANES
    tspec = pl.BlockSpec((None, tt, LANES), lambda i, g: (i // nb, i % nb, 0))
    if stacked_groups is None:
        grid = (T // tt, 1)
        out_spec = pl.BlockSpec((tt, width), lambda i, g: (i, 0))
        out_shape = jax.ShapeDtypeStruct((T, width), BF16)
    else:
        grid = (T // tt, stacked_groups)
        out_spec = pl.BlockSpec((None, None, n_heads, tt, LANES), lambda i, g: (i, g, 0, 0, 0))
        out_shape = jax.ShapeDtypeStruct((T // tt, stacked_groups, n_heads, tt, LANES), BF16)
    return pl.pallas_call(
        functools.partial(_rope_kernel, half=half, n_heads=n_heads, stacked=stacked_groups is not None),
        grid=grid,
        in_specs=[pl.BlockSpec((tt, width), lambda i, g: (i, col0 + g)), tspec, tspec, tspec],
        out_specs=out_spec,
        out_shape=out_shape,
        compiler_params=_cparams(("arbitrary", "arbitrary")),
        name="rope",
    )(p, *tables)


def _compress_kernel(t_ref, pe_ref, w1_ref, w2_ref, o_ref):
    t = t_ref[...]
    half = t.shape[1]
    w1a = w1_ref[:half, :]
    w1b = w1_ref[half:, :]
    pe = pe_ref[...]
    bias = _dot3(jnp.broadcast_to(pe[:, :half], (8, half)), w1a.astype(F32)) + _dot3(
        jnp.broadcast_to(pe[:, half:], (8, half)), w1b.astype(F32))
    a = _dot(t, w1a)
    b = _dot(t, w1b)
    n = a.shape[0]
    hid = a + pltpu.roll(b, n - 1, 0) + bias[0:1, :]
    o_ref[...] = _dot(_gelu_tanh(hid).astype(BF16), w2_ref[...]).astype(o_ref.dtype)


def nsa_compress(t2, pe, w1, w2):
    B, G, M, K = t2.shape
    d = w2.shape[1]
    return pl.pallas_call(
        _compress_kernel,
        grid=(B, G),
        in_specs=[
            pl.BlockSpec((None, None, M, K), lambda b, g: (b, g, 0, 0)),
            pl.BlockSpec((1, 2 * K), lambda b, g: (0, 0)),
            pl.BlockSpec((2 * K, w1.shape[1]), lambda b, g: (0, 0)),
            pl.BlockSpec(w2.shape, lambda b, g: (0, 0)),
        ],
        out_specs=pl.BlockSpec((None, None, M, d), lambda b, g: (b, g, 0, 0)),
        out_shape=jax.ShapeDtypeStruct((B, G, M, d), BF16),
        compiler_params=_cparams(("arbitrary", "arbitrary")),
        name="nsa_compress",
    )(t2, pe.reshape(1, 2 * K), w1.astype(BF16), w2.astype(BF16))


CMP_Q_BLOCKS = 8


def _cmp_select_kernel(q_ref, kc_ref, vc_ref, ov_ref, oc_ref, bias_ref, *, nblk, **kw):
    for b in range(nblk):
        _cmp_select_block(pl.program_id(2) * nblk + b, q_ref.at[b], kc_ref, vc_ref, ov_ref, oc_ref.at[b], bias_ref.at[b], **kw)


def _cmp_select_block(i, q_ref, kc_ref, vc_ref, ov_ref, oc_ref, bias_ref, *, qt, n_cmp, n_slc, slc_k, scale):
    rb = q_ref.shape[0]
    R = rb // qt
    ncp = kc_ref.shape[0]
    q = q_ref[...]
    tok = i * qt + lax.broadcasted_iota(jnp.int32, (rb, 1), 0) % qt
    n_id = lax.broadcasted_iota(jnp.int32, (1, ncp), 1)
    valid = ((n_id * CMP_STRIDE + (CMP_BLK - 1)) <= tok) & (n_id < n_cmp)
    s = _dot_nt(q, kc_ref[...]) * scale
    m = jnp.max(jnp.where(valid, s, NEG), axis=-1, keepdims=True)
    e = jnp.where(valid, jnp.exp(s - m), 0.0)
    p = e / jnp.maximum(jnp.sum(e, axis=-1, keepdims=True), TINY)
    oc_ref[...] = _dot(p.astype(BF16), vc_ref[...]).astype(oc_ref.dtype)
    psum = p[0:qt]
    for r in range(1, R):
        psum = psum + p[r * qt:(r + 1) * qt]
    hi, lo = _split(psum)
    ov = ov_ref[...]
    imp = _dot(hi, ov) + _dot(lo, ov)
    t1 = i * qt + lax.broadcasted_iota(jnp.int32, (qt, 1), 0)
    blk = lax.broadcasted_iota(jnp.int32, (1, LANES), 1)
    cur = t1 // SLC_BLK
    forced = (blk == 0) | (blk == cur) | (blk == cur - 1)
    ok = blk * SLC_BLK <= t1
    score = jnp.where(forced, BIG, jnp.where(ok, imp, -BIG))
    score = jnp.where(blk < n_slc, score, -jnp.inf)
    bias = jnp.full((qt, LANES), -BIG, F32)
    for _ in range(slc_k):
        mx = jnp.max(score, axis=-1, keepdims=True)
        idx = jnp.min(jnp.where(score == mx, blk, LANES), axis=-1, keepdims=True)
        hit = blk == idx
        bias = jnp.where(hit, 0.0, bias)
        score = jnp.where(hit, -jnp.inf, score)
    bias_ref[...] = bias.astype(bias_ref.dtype)


def nsa_cmp_select(q_st, kc_c, vc_c, overlap, *, qt, n_cmp, n_slc, slc_k, scale):
    B, G, nQ, rb, d = q_st.shape
    ncp = kc_c.shape[2]
    nblk = CMP_Q_BLOCKS if nQ % CMP_Q_BLOCKS == 0 else 1
    return pl.pallas_call(
        functools.partial(_cmp_select_kernel, nblk=nblk, qt=qt, n_cmp=n_cmp, n_slc=n_slc, slc_k=slc_k, scale=scale),
        grid=(B, G, nQ // nblk),
        in_specs=[
            pl.BlockSpec((None, None, nblk, rb, d), lambda b, g, i: (b, g, i, 0, 0)),
            pl.BlockSpec((None, None, ncp, d), lambda b, g, i: (b, g, 0, 0)),
            pl.BlockSpec((None, None, ncp, d), lambda b, g, i: (b, g, 0, 0)),
            pl.BlockSpec((ncp, LANES), lambda b, g, i: (0, 0)),
        ],
        out_specs=[
            pl.BlockSpec((None, None, nblk, rb, d), lambda b, g, i: (b, g, i, 0, 0)),
            pl.BlockSpec((None, None, nblk, qt, LANES), lambda b, g, i: (b, g, i, 0, 0)),
        ],
        out_shape=[
            jax.ShapeDtypeStruct((B, G, nQ, rb, d), BF16),
            jax.ShapeDtypeStruct((B, G, nQ, qt, LANES), BF16),
        ],
        compiler_params=_cparams(("arbitrary", "arbitrary", "arbitrary")),
        name="nsa_cmp_select",
    )(q_st, kc_c, vc_c, overlap)


def _gate_combine_kernel(oc_ref, os_ref, ow_ref, gl_ref, o_ref, *, R):
    gl = gl_ref[...]
    gs = jax.nn.sigmoid(gl)
    g = pl.program_id(1)
    for r in range(R):
        outs = []
        acc = None
        for br, ref in enumerate((oc_ref, os_ref, ow_ref)):
            lane = (g * R + r) * 3 + br
            sel = lax.broadcasted_iota(jnp.int32, gs.shape, 1) == lane
            gv = jnp.sum(jnp.where(sel, gs, 0.0), axis=-1, keepdims=True)
            term = gv * ref[r].astype(F32)
            acc = term if acc is None else acc + term
        o_ref[:, r * LANES:(r + 1) * LANES] = acc.astype(o_ref.dtype)


def nsa_gate_combine(oc, osl, ow, gl, *, R):
    nT, G, _, tt, d = oc.shape
    T = nT * tt
    ospec = pl.BlockSpec((None, None, R, tt, d), lambda i, g: (i, g, 0, 0, 0))
    return pl.pallas_call(
        functools.partial(_gate_combine_kernel, R=R),
        grid=(nT, G),
        in_specs=[ospec, ospec, ospec, pl.BlockSpec((tt, LANES), lambda i, g: (i, 0))],
        out_specs=pl.BlockSpec((tt, R * d), lambda i, g: (i, g)),
        out_shape=jax.ShapeDtypeStruct((T, G * R * d), BF16),
        compiler_params=_cparams(("arbitrary", "arbitrary")),
        name="nsa_gate_combine",
    )(oc, osl, ow, gl)


def _topk_rows(s, k):
    n = s.shape[0]
    iota = lax.broadcasted_iota(jnp.int32, s.shape, 0)
    vals, idxs = [], []
    for _ in range(k):
        m = jnp.max(s, axis=0, keepdims=True)
        idx = jnp.min(jnp.where(s == m, iota, n), axis=0, keepdims=True)
        vals.append(m)
        idxs.append(idx)
        s = jnp.where(iota == idx, -jnp.inf, s)
    return jnp.concatenate(vals, axis=0), jnp.concatenate(idxs, axis=0)


def _peer_route_kernel(q_ref, k1_ref, k2_ref, e_ref, g_ref, *, topk, n_keys):
    half = k1_ref.shape[1]
    s1 = _dot3_nt(k1_ref[...], q_ref[:, :half])
    s2 = _dot3_nt(k2_ref[...], q_ref[:, half:])
    v1, i1 = _topk_rows(s1, topk)
    v2, i2 = _topk_rows(s2, topk)
    assert topk == 2 * SUBLANES
    tb = s1.shape[1]
    sub = lax.broadcasted_iota(jnp.int32, (SUBLANES, tb), 0)
    cand, cidx, cpos = [], [], []
    for a in range(SUBLANES):
        for m in range(2 if a == 0 else 1):
            bs = slice(m * SUBLANES, (m + 1) * SUBLANES)
            cand.append(v1[a:a + 1, :] + v2[bs, :])
            cidx.append(i1[a:a + 1, :] * n_keys + i2[bs, :])
            cpos.append(a * topk + m * SUBLANES + sub)
    cand.append(v1[SUBLANES:, :] + v2[0:1, :])
    cidx.append(i1[SUBLANES:, :] * n_keys + i2[0:1, :])
    cpos.append((SUBLANES + sub) * topk)
    cand = jnp.concatenate(cand, axis=0)
    cidx = jnp.concatenate(cidx, axis=0)
    cpos = jnp.concatenate(cpos, axis=0)
    tops, exs = [], []
    for _ in range(topk):
        m = jnp.max(cand, axis=0, keepdims=True)
        p = jnp.min(jnp.where(cand == m, cpos, topk * topk), axis=0, keepdims=True)
        hit = cpos == p
        tops.append(m)
        exs.append(jnp.sum(jnp.where(hit, cidx, 0), axis=0, keepdims=True))
        cand = jnp.where(hit, -jnp.inf, cand)
    top = jnp.concatenate(tops, axis=0)
    w = jnp.exp(top - top[0:1])
    e_ref[...] = jnp.concatenate(exs, axis=0)
    g_ref[...] = w / jnp.sum(w, axis=0, keepdims=True)


def peer_route(q, k1, k2, *, heads, topk, tb=1024):
    T, N = q.shape
    n_keys, half = k1.shape
    hk = heads * topk
    tb = min(tb, T)
    return pl.pallas_call(
        functools.partial(_peer_route_kernel, topk=topk, n_keys=n_keys),
        grid=(T // tb, heads),
        in_specs=[
            pl.BlockSpec((tb, 2 * half), lambda i, h: (i, h)),
            pl.BlockSpec((n_keys, half), lambda i, h: (0, 0)),
            pl.BlockSpec((n_keys, half), lambda i, h: (0, 0)),
        ],
        out_specs=[pl.BlockSpec((topk, tb), lambda i, h: (h, i)), pl.BlockSpec((topk, tb), lambda i, h: (h, i))],
        out_shape=[jax.ShapeDtypeStruct((hk, T), jnp.int32), jax.ShapeDtypeStruct((hk, T), F32)],
        compiler_params=_cparams(("arbitrary", "arbitrary")),
        name="peer_route",
    )(q, k1, k2)


def _pack_kernel(u_ref, v_ref, o_ref):
    ub = pltpu.bitcast(u_ref[...].astype(BF16).astype(F32), jnp.uint32) >> 16
    vb = pltpu.bitcast(v_ref[...].astype(BF16).astype(F32), jnp.uint32) & jnp.uint32(0xFFFF0000)
    o_ref[...] = vb | ub


def peer_pack(u, v, layer, te=512):
    _, E, D = u.shape
    in_spec = pl.BlockSpec((None, te, D), lambda i: (layer, i, 0))
    spec = pl.BlockSpec((te, D), lambda i: (i, 0))
    return pl.pallas_call(
        _pack_kernel,
        grid=(E // te,),
        in_specs=[in_spec, in_spec],
        out_specs=spec,
        out_shape=jax.ShapeDtypeStruct((E, D), jnp.uint32),
        compiler_params=_cparams(("arbitrary",)),
        name="peer_pack",
    )(u, v)


def _peer_mix_kernel(idx_ref, idxn_ref, h_ref, g_ref, x_ref, gate_ref, tbl_ref, o_ref, buf_ref, sem_ref, *, tb, nk):
    i = pl.program_id(0)
    n = pl.num_programs(0)
    rows = tb * nk
    nlt = buf_ref.shape[1]
    slot = i % 2
    hi_mask = jnp.uint32(0xFFFF0000)

    def row_copy(src_idx_ref, r, dst_slot):
        return pltpu.make_async_copy(tbl_ref.at[src_idx_ref[0, r]], buf_ref.at[dst_slot, :, r, :], sem_ref.at[dst_slot])

    def slot_copy(s):
        return pltpu.make_async_copy(buf_ref.at[s], buf_ref.at[s], sem_ref.at[s])

    @pl.when(i == 0)
    def _():
        def one(r, c):
            row_copy(idx_ref, r, 0).start()
            return c
        lax.fori_loop(0, rows, one, 0, unroll=8)

    slot_copy(slot).wait()

    ys = []
    for t in range(tb):
        hb = [jnp.broadcast_to(h_ref[t:t + 1, lt * LANES:(lt + 1) * LANES], (SUBLANES, LANES)) for lt in range(nlt)]
        yacc = [None] * nlt
        for c in range(nk // SUBLANES):
            r0 = t * nk + c * SUBLANES
            ws = [buf_ref[slot, lt, r0:r0 + SUBLANES, :] for lt in range(nlt)]
            s = None
            for lt in range(nlt):
                term = pltpu.bitcast(ws[lt] << 16, F32) * hb[lt]
                s = term if s is None else s + term
            act = jnp.sum(s, axis=-1, keepdims=True)
            a = g_ref[c * SUBLANES:(c + 1) * SUBLANES, t:t + 1] * _gelu_tanh(act)
            for lt in range(nlt):
                yv = a * pltpu.bitcast(ws[lt] & hi_mask, F32)
                yacc[lt] = yv if yacc[lt] is None else yacc[lt] + yv
            for r in range(r0, r0 + SUBLANES):
                row_copy(idxn_ref, r, 1 - slot).start(priority=r % 2)
        ys.append(jnp.concatenate([jnp.sum(ya, axis=0, keepdims=True) for ya in yacc], axis=1))
    o_ref[...] = x_ref[...] + gate_ref[...] * jnp.concatenate(ys, axis=0)

    @pl.when(i == n - 1)
    def _():
        slot_copy(1 - slot).wait()


def peer_mix(experts_t, g_t, h, x, gate, table, *, seq, tb=8):
    T, D = h.shape
    nk = experts_t.shape[0]
    B = gate.shape[0]
    nb = seq // tb
    nblk = T // tb
    idx = experts_t.T.reshape(nblk, 1, tb * nk)
    g = g_t.reshape(nk, nblk, tb).transpose(1, 0, 2)
    smem_spec = lambda f: pl.BlockSpec((None, 1, tb * nk), f, memory_space=pltpu.SMEM)
    return pl.pallas_call(
        functools.partial(_peer_mix_kernel, tb=tb, nk=nk),
        grid=(nblk,),
        in_specs=[
            smem_spec(lambda i: (i, 0, 0)),
            smem_spec(lambda i: (jnp.minimum(i + 1, nblk - 1), 0, 0)),
            pl.BlockSpec((tb, D), lambda i: (i, 0)),
            pl.BlockSpec((None, nk, tb), lambda i: (i, 0, 0)),
            pl.BlockSpec((tb, D), lambda i: (i, 0)),
            pl.BlockSpec((None, 1, D), lambda i: (i // nb, 0, 0)),
            pl.BlockSpec(memory_space=pl.ANY),
        ],
        out_specs=pl.BlockSpec((tb, D), lambda i: (i, 0)),
        out_shape=jax.ShapeDtypeStruct((T, D), F32),
        scratch_shapes=[pltpu.VMEM((2, D // LANES, tb * nk, LANES), jnp.uint32), pltpu.SemaphoreType.DMA((2,))],
        compiler_params=_cparams(("arbitrary",)),
        name="peer_mix",
    )(idx, idx, h, g, x, gate.reshape(B, 1, D), table)


def _rmsnorm_kernel(x_ref, g_ref, o_ref):
    x = x_ref[...]
    o_ref[...] = (x * lax.rsqrt(jnp.mean(x * x, axis=-1, keepdims=True) + EPS)) * g_ref[...]


def rmsnorm_rows(x, g, tm=512):
    T, D = x.shape
    return pl.pallas_call(
        _rmsnorm_kernel,
        grid=(T // tm,),
        in_specs=[pl.BlockSpec((tm, D), lambda i: (i, 0)), pl.BlockSpec((1, D), lambda i: (0, 0))],
        out_specs=pl.BlockSpec((tm, D), lambda i: (i, 0)),
        out_shape=jax.ShapeDtypeStruct((T, D), F32),
        compiler_params=_cparams(("arbitrary",)),
        name="final_rmsnorm",
    )(x, g.reshape(1, D))


def _split3(m):
    d = m.shape[-1] // 3
    return m[:, :d], m[:, d:2 * d], m[:, 2 * d:]


def sb_mla_layer(x, mod, pos, g_norm, w_in, q_norm, w_uq, kv_norm, w_ukv, w_out, *, batch, seq):
    T, D = x.shape
    shift, scale, gate = _split3(mod)
    sbw = SB_HEADS * SB_DIM
    n_in = w_in.shape[1]
    n_pad = -(-n_in // 512) * 512
    p = norm_matmul(x, g_norm, scale, shift, _pad_cols(w_in, n_pad).astype(BF16), seq=seq, tm=512, tn=512)
    o_a = stick_breaking_attn(p, batch=batch, seq=seq, heads=SB_HEADS, dim=SB_DIM,
                              q_col=0, k_col=SB_HEADS, v_col=2 * SB_HEADS)
    c_q = p[:, 3 * sbw:3 * sbw + MLA_Q_RANK]
    c_kv = p[:, 3 * sbw + MLA_Q_RANK:3 * sbw + MLA_Q_RANK + MLA_KV_RANK]
    k_r = p[:, 3 * sbw + MLA_Q_RANK + MLA_KV_RANK:n_in]
    zq = jnp.zeros((batch, MLA_Q_RANK), F32)
    zkv = jnp.zeros((batch, MLA_KV_RANK), F32)
    dqk = MLA_NOPE + MLA_ROPE
    wq = w_uq.reshape(MLA_Q_RANK, MLA_HEADS, dqk)
    wq = jnp.concatenate([wq[:, :, :MLA_NOPE].reshape(MLA_Q_RANK, -1), wq[:, :, MLA_NOPE:].reshape(MLA_Q_RANK, -1)], 1)
    wkv = w_ukv.reshape(MLA_KV_RANK, MLA_HEADS, MLA_NOPE + MLA_V)
    wkv = jnp.concatenate([wkv[:, :, :MLA_NOPE].reshape(MLA_KV_RANK, -1), wkv[:, :, MLA_NOPE:].reshape(MLA_KV_RANK, -1)], 1)
    qf = norm_matmul(c_q, q_norm, zq, zq, wq.astype(BF16), seq=seq, tm=512, tn=512)
    kvf = norm_matmul(c_kv, kv_norm, zkv, zkv, wkv.astype(BF16), seq=seq, tm=512, tn=512, out_dtype=BF16)
    tables = _rope_tables(pos, MLA_ROPE, MLA_ROPE)
    nope_w = MLA_HEADS * MLA_NOPE
    q_rope = rope_cols(qf, tables, seq=seq, col0=nope_w // (MLA_HEADS * MLA_ROPE), n_heads=MLA_HEADS * MLA_ROPE // LANES,
                       half=MLA_ROPE // 2, tt=512)
    kr_pad = jnp.pad(k_r, ((0, 0), (0, LANES - MLA_ROPE)))
    k_rope = rope_cols(kr_pad, tables, seq=seq, col0=0, n_heads=1, half=MLA_ROPE // 2, tt=512)[:, :MLA_ROPE]
    H = MLA_HEADS
    q_nope = qf[:, :nope_w].astype(BF16).reshape(batch, seq, H, MLA_NOPE)
    q_cat = jnp.concatenate([q_nope, q_rope.reshape(batch, seq, H, MLA_ROPE)], -1)
    tq = min(1024, seq)
    q_cat = q_cat.transpose(0, 2, 1, 3).reshape(batch * H, seq // tq, tq, dqk)
    k_nope = kvf[:, :nope_w].reshape(batch, seq, H, MLA_NOPE)
    k_cat = jnp.concatenate([k_nope, jnp.broadcast_to(k_rope.reshape(batch, seq, 1, MLA_ROPE), (batch, seq, H, MLA_ROPE))], -1)
    k_cat = k_cat.transpose(0, 2, 1, 3).reshape(batch * H, seq, dqk)
    v = kvf[:, nope_w:].reshape(batch, seq, H, MLA_V).transpose(0, 2, 1, 3).reshape(batch * H, seq, MLA_V)
    o_b = flash_attn(q_cat, k_cat, v, qt=tq, tk=256, scale=1.0 / math.sqrt(dqk))
    o_b = o_b.reshape(batch, H, seq, MLA_V).transpose(0, 2, 1, 3).reshape(T, H * MLA_V)
    wo = w_out.astype(BF16)
    return matmul_residual([o_a, o_b], [wo[:sbw], wo[sbw:]], x, gate, seq=seq, tm=512, tn=min(512, D))


def nsa_layer(x, mod, pos, g_norm, w_in, pe_k, pe_v, w1_k, w2_k, w1_v, w2_v, w_out, *, batch, seq):
    T, D = x.shape
    shift, scale, gate = _split3(mod)
    G, R, d = NSA_GROUPS, NSA_HEADS // NSA_GROUPS, NSA_DIM
    H = NSA_HEADS
    kvw = G * d
    n_in = w_in.shape[1]
    n_pad = -(-n_in // 768) * 768
    p = norm_matmul(x, g_norm, scale, shift, _pad_cols(w_in, n_pad).astype(BF16), seq=seq, tm=512, tn=768)
    tables = _rope_tables(pos, ROT_DIM, d)
    half = ROT_DIM // 2
    qt = QBLK
    nQ = seq // qt
    q_st = rope_cols(p, tables, seq=seq, col0=0, n_heads=R, half=half, tt=qt, stacked_groups=G)
    q_st = q_st.reshape(batch, nQ, G, R * qt, d).transpose(0, 2, 1, 3, 4)
    base = H * d // kvw
    kc = rope_cols(p, tables, seq=seq, col0=base + 0, n_heads=G, half=half, tt=512)
    ks = rope_cols(p, tables, seq=seq, col0=base + 2, n_heads=G, half=half, tt=512)
    kw = rope_cols(p, tables, seq=seq, col0=base + 4, n_heads=G, half=half, tt=512)
    off = H * d
    vc = p[:, off + kvw:off + 2 * kvw].astype(BF16)
    vs = p[:, off + 3 * kvw:off + 4 * kvw].astype(BF16)
    vw = p[:, off + 5 * kvw:off + 6 * kvw].astype(BF16)
    gl = jnp.pad(p[:, off + 6 * kvw:n_in], ((0, 0), (0, LANES - 3 * H)))

    def per_group(a):
        return a.reshape(batch, seq, G, d).transpose(0, 2, 1, 3)

    M = seq // CMP_STRIDE
    n_cmp = (seq - CMP_BLK) // CMP_STRIDE + 1
    ncp = -(-M // LANES) * LANES
    def chunks(a):
        c = per_group(a).reshape(batch, G, M, CMP_STRIDE * d)
        return jnp.pad(c, ((0, 0), (0, 0), (0, ncp - M), (0, 0)))
    kc_c = nsa_compress(chunks(kc), pe_k, w1_k, w2_k)
    vc_c = nsa_compress(chunks(vc), pe_v, w1_v, w2_v)
    n_slc = seq // SLC_BLK
    slc_k = min(SLC_TOPK, n_slc)
    c_s = np.arange(ncp) * CMP_STRIDE
    s_s = np.arange(LANES) * SLC_BLK
    ovl = np.clip(np.minimum(c_s[:, None] + CMP_BLK, s_s[None, :] + SLC_BLK) - np.maximum(c_s[:, None], s_s[None, :]), 0, None)
    ovl[n_cmp:, :] = 0
    ovl[:, n_slc:] = 0
    scale_a = 1.0 / math.sqrt(d)
    oc, bias = nsa_cmp_select(q_st, kc_c, vc_c, jnp.asarray(ovl, BF16), qt=qt, n_cmp=n_cmp, n_slc=n_slc,
                              slc_k=slc_k, scale=scale_a)
    q_aug = jnp.concatenate([q_st, jnp.tile(bias, (1, 1, 1, R, 1))], -1).reshape(batch * G, nQ, R * qt, 2 * d)
    onehot = (np.arange(seq)[:, None] // SLC_BLK == np.arange(LANES)[None, :]).astype(np.float32)
    ks_g = per_group(ks)
    k_aug = jnp.concatenate([ks_g, jnp.broadcast_to(jnp.asarray(onehot, BF16), ks_g.shape[:2] + onehot.shape)], -1)
    k_aug = k_aug.reshape(batch * G, seq, 2 * d)
    osl = flash_attn(q_aug, k_aug, per_group(vs).reshape(batch * G, seq, d), qt=qt, tk=256, scale=scale_a)
    q_flat = q_st.reshape(batch * G, nQ, R * qt, d)
    ow = flash_attn(q_flat, per_group(kw).reshape(batch * G, seq, d), per_group(vw).reshape(batch * G, seq, d),
                    qt=qt, tk=128, scale=scale_a, window=WINDOW)

    def unstack(o):
        return o.reshape(batch, G, nQ, R, qt, d).transpose(0, 2, 1, 3, 4, 5).reshape(batch * nQ, G, R, qt, d)

    o = nsa_gate_combine(unstack(oc.reshape(batch * G, nQ, R * qt, d)), unstack(osl), unstack(ow), gl, R=R)
    return matmul_residual([o], [w_out.astype(BF16)], x, gate, seq=seq, tm=512, tn=min(512, D))


def peer_layer(x, mod, g_norm, w_q, k1, k2, u_all, v_all, layer, *, batch, seq):
    shift, scale, gate = _split3(mod)
    q, h = norm_matmul(x, g_norm, scale, shift, w_q.astype(BF16), seq=seq, tm=512, tn=512, emit_h=True)
    experts, g = peer_route(q, k1, k2, heads=PEER_HEADS, topk=PEER_TOPK)
    table = peer_pack(u_all, v_all, layer).reshape(u_all.shape[1], u_all.shape[2] // LANES, LANES)
    return peer_mix(experts, g, h, x, gate, table, seq=seq)


def kernel(x, c, positions, norm_mix, ada_mix_w, ada_mix_b, sbmla_w_in, mla_q_norm, mla_w_uq, mla_kv_norm, mla_w_ukv, sbmla_w_out, nsa_w_in, nsa_pe_k, nsa_pe_v, nsa_w1_k, nsa_w2_k, nsa_w1_v, nsa_w2_v, nsa_w_out, norm_ffn, ada_ffn_w, ada_ffn_b, peer_w_q, peer_k1, peer_k2, peer_u, peer_v, final_norm):
    B, S, D = x.shape
    depth = norm_mix.shape[0]
    mod_mix = modulation_all(c, ada_mix_w, ada_mix_b)
    mod_ffn = modulation_all(c, ada_ffn_w, ada_ffn_b)
    xs = x.reshape(B * S, D)
    for layer in range(depth):
        i = layer // 2
        if layer % 2 == 0:
            xs = sb_mla_layer(xs, mod_mix[layer], positions, norm_mix[layer], sbmla_w_in[i], mla_q_norm[i], mla_w_uq[i],
                              mla_kv_norm[i], mla_w_ukv[i], sbmla_w_out[i], batch=B, seq=S)
        else:
            xs = nsa_layer(xs, mod_mix[layer], positions, norm_mix[layer], nsa_w_in[i], nsa_pe_k[i], nsa_pe_v[i],
                           nsa_w1_k[i], nsa_w2_k[i], nsa_w1_v[i], nsa_w2_v[i], nsa_w_out[i], batch=B, seq=S)
        xs = peer_layer(xs, mod_ffn[layer], norm_ffn[layer], peer_w_q[layer], peer_k1[layer], peer_k2[layer],
                        peer_u, peer_v, layer, batch=B, seq=S)
    return rmsnorm_rows(xs, final_norm).reshape(B, S, D)
```

```python
import functools
import math

import numpy as np
import jax
import jax.numpy as jnp
from jax import lax
from jax.experimental import pallas as pl
from jax.experimental.pallas import tpu as pltpu

F32 = jnp.float32
BF16 = jnp.bfloat16

QBLK = 128
ROPE_THETA = 500000.0
EPS = 1e-6
TINY = 1e-30
BIG = 1e9
NEG = -1e30

SB_HEADS = 8
SB_DIM = 128
MLA_HEADS = 8
MLA_Q_RANK = 512
MLA_KV_RANK = 256
MLA_NOPE = 128
MLA_ROPE = 64
MLA_V = 128
NSA_HEADS = 16
NSA_GROUPS = 2
NSA_DIM = 128
ROT_DIM = NSA_DIM // 4
CMP_BLK = 32
CMP_STRIDE = 16
CMP_HIDDEN = 256
SLC_BLK = 64
SLC_TOPK = 16
WINDOW = 512
PEER_HEADS = 8
PEER_KEYS = 128
PEER_DKEY = 256
PEER_TOPK = 16

LANES = 128
SUBLANES = 8
VMEM_LIMIT = 56 * 1024 * 1024


def _cparams(sem):
    return pltpu.CompilerParams(dimension_semantics=sem, vmem_limit_bytes=VMEM_LIMIT)


def _split(a):
    hi = a.astype(BF16)
    lo = (a - hi.astype(F32)).astype(BF16)
    return hi, lo


def _dot(a, b):
    return jnp.dot(a, b, preferred_element_type=F32)


def _dot_nt(a, b):
    return lax.dot_general(a, b, (((1,), (1,)), ((), ())), preferred_element_type=F32)


def _dot3(a, b):
    ah, al = _split(a)
    bh, bl = _split(b)
    return _dot(ah, bh) + (_dot(ah, bl) + _dot(al, bh))


def _dot3_nt(a, b):
    ah, al = _split(a)
    bh, bl = _split(b)
    return _dot_nt(ah, bh) + (_dot_nt(ah, bl) + _dot_nt(al, bh))


def _gelu_tanh(x):
    return 0.5 * x * (1.0 + jnp.tanh(math.sqrt(2.0 / math.pi) * (x + 0.044715 * (x * x * x))))


def _pad_cols(w, n):
    return jnp.pad(w, ((0, 0), (0, n - w.shape[1])))


def _mod_kernel(c_ref, w_ref, b_ref, o_ref):
    c = c_ref[...]
    s = c * jax.nn.sigmoid(c)
    o_ref[0] = _dot3(s, w_ref[0]) + b_ref[0]


def modulation_all(c, w, b):
    L, D, N = w.shape
    B = c.shape[0]
    rows = 8
    cp = jnp.pad(c, ((0, rows - B), (0, 0)))
    tn = 768 if N % 768 == 0 else N
    out = pl.pallas_call(
        _mod_kernel,
        grid=(L, N // tn),
        in_specs=[
            pl.BlockSpec((rows, D), lambda l, j: (0, 0)),
            pl.BlockSpec((1, D, tn), lambda l, j: (l, 0, j)),
            pl.BlockSpec((1, 1, tn), lambda l, j: (l, 0, j)),
        ],
        out_specs=pl.BlockSpec((1, rows, tn), lambda l, j: (l, 0, j)),
        out_shape=jax.ShapeDtypeStruct((L, rows, N), F32),
        compiler_params=_cparams(("arbitrary", "arbitrary")),
        name="adaln_mod",
    )(cp, w, b.reshape(L, 1, N))
    return out[:, :B]


def _norm_mm_kernel(x_ref, g_ref, sc_ref, sh_ref, w_ref, o_ref, *rest, emit_h):
    if emit_h:
        h_ref, hb_ref = rest
    else:
        (hb_ref,) = rest

    @pl.when(pl.program_id(1) == 0)
    def _():
        x = x_ref[...]
        y = x * lax.rsqrt(jnp.mean(x * x, axis=-1, keepdims=True) + EPS)
        h = (y * g_ref[...]) * (1.0 + sc_ref[...]) + sh_ref[...]
        hb_ref[...] = h.astype(BF16)
        if emit_h:
            h_ref[...] = h

    o_ref[...] = _dot(hb_ref[...], w_ref[...]).astype(o_ref.dtype)


def norm_matmul(x, g, scale, shift, w, *, seq, tm, tn, emit_h=False, out_dtype=F32):
    T, K = x.shape
    N = w.shape[1]
    assert T % tm == 0 and N % tn == 0 and seq % tm == 0
    nb = seq // tm
    B = scale.shape[0]
    out_shape = [jax.ShapeDtypeStruct((T, N), out_dtype)]
    out_specs = [pl.BlockSpec((tm, tn), lambda i, j: (i, j))]
    if emit_h:
        out_shape.append(jax.ShapeDtypeStruct((T, K), F32))
        out_specs.append(pl.BlockSpec((tm, K), lambda i, j: (i, 0)))
    res = pl.pallas_call(
        functools.partial(_norm_mm_kernel, emit_h=emit_h),
        grid=(T // tm, N // tn),
        in_specs=[
            pl.BlockSpec((tm, K), lambda i, j: (i, 0)),
            pl.BlockSpec((1, K), lambda i, j: (0, 0)),
            pl.BlockSpec((None, 1, K), lambda i, j: (i // nb, 0, 0)),
            pl.BlockSpec((None, 1, K), lambda i, j: (i // nb, 0, 0)),
            pl.BlockSpec((K, tn), lambda i, j: (0, j)),
        ],
        out_specs=out_specs,
        out_shape=out_shape,
        scratch_shapes=[pltpu.VMEM((tm, K), BF16)],
        compiler_params=_cparams(("arbitrary", "arbitrary")),
        name="norm_matmul",
    )(x, g.reshape(1, K), scale.reshape(B, 1, K), shift.reshape(B, 1, K), w)
    return res if emit_h else res[0]


def _mm_res_kernel(*refs, n_in):
    a_refs = refs[:n_in]
    w_refs = refs[n_in:2 * n_in]
    x_ref, gate_ref, o_ref = refs[2 * n_in:]
    y = _dot(a_refs[0][...], w_refs[0][...])
    for a_ref, w_ref in zip(a_refs[1:], w_refs[1:]):
        y = y + _dot(a_ref[...], w_ref[...])
    o_ref[...] = x_ref[...] + gate_ref[...] * y


def matmul_residual(a_list, w_list, x, gate, *, seq, tm, tn):
    T, D = x.shape
    B = gate.shape[0]
    nb = seq // tm
    n_in = len(a_list)
    in_specs = [pl.BlockSpec((tm, a.shape[1]), lambda i, j: (i, 0)) for a in a_list]
    in_specs += [pl.BlockSpec((w.shape[0], tn), lambda i, j: (0, j)) for w in w_list]
    in_specs += [
        pl.BlockSpec((tm, tn), lambda i, j: (i, j)),
        pl.BlockSpec((None, 1, tn), lambda i, j: (i // nb, 0, j)),
    ]
    return pl.pallas_call(
        functools.partial(_mm_res_kernel, n_in=n_in),
        grid=(T // tm, D // tn),
        in_specs=in_specs,
        out_specs=pl.BlockSpec((tm, tn), lambda i, j: (i, j)),
        out_shape=jax.ShapeDtypeStruct((T, D), F32),
        compiler_params=_cparams(("arbitrary", "arbitrary")),
        name="matmul_residual",
    )(*a_list, *w_list, x, gate.reshape(B, 1, D))


LOG2E = 1.4426950408889634


def _sb_kernel(q_ref, k_ref, v_ref, o_ref, c_ref, acc_ref, za_ref, zb_ref, qb_ref, *, tq, tk, scale):
    i = pl.program_id(1)
    qb_ref[...] = q_ref[...].astype(BF16)
    t_idx = i * tq + lax.broadcasted_iota(jnp.int32, (tq, 1), 0)
    col = lax.broadcasted_iota(jnp.int32, (1, tk), 1)
    tri = (lax.broadcasted_iota(jnp.int32, (tk, tk), 0) > lax.broadcasted_iota(jnp.int32, (tk, tk), 1)).astype(BF16)
    tri2 = jnp.concatenate([tri, tri], axis=0)
    c_ref[...] = jnp.zeros(c_ref.shape, F32)
    acc_ref[...] = jnp.zeros(acc_ref.shape, F32)
    top = 2 * i + 1

    def tile_start(n):
        return pl.multiple_of(jnp.maximum(top - n, 0) * tk, tk)

    def scores(n, z_ref):
        z_ref[...] = _dot_nt(qb_ref[...], k_ref[pl.ds(tile_start(n), tk), :].astype(BF16))

    def update(n, z_ref, masked):
        v = v_ref[pl.ds(tile_start(n), tk), :].astype(BF16)
        z = z_ref[...] * (scale * LOG2E)
        sp = jnp.maximum(z, 0.0) + jnp.log2(1.0 + jnp.exp2(-jnp.abs(z)))
        if masked:
            strict = ((top - n) * tk + col) < t_idx
            lp = jnp.where(strict, sp, 0.0)
        else:
            lp = sp
        hi, lo = _split(lp)
        suf = _dot(jnp.concatenate([hi, lo], axis=1), tri2)
        c = c_ref[...]
        a = jnp.exp2((z - sp) - (suf + c))
        if masked:
            a = jnp.where(strict, a, 0.0)
        acc_ref[...] += _dot(a.astype(BF16), v)
        c_ref[...] = c + jnp.sum(lp, axis=-1, keepdims=True)

    scores(0, za_ref)
    scores(1, zb_ref)
    update(0, za_ref, True)
    scores(2, za_ref)
    update(1, zb_ref, True)

    def body(m, carry):
        n = 2 * m
        scores(n + 1, zb_ref)
        update(n, za_ref, False)
        scores(n + 2, za_ref)
        update(n + 1, zb_ref, False)
        return carry

    lax.fori_loop(1, i + 1, body, 0)
    o_ref[...] = acc_ref[...].astype(o_ref.dtype)


def stick_breaking_attn(p, *, batch, seq, heads, dim, q_col, k_col, v_col, tq=512):
    nq = seq // tq
    tk = tq // 2
    return pl.pallas_call(
        functools.partial(_sb_kernel, tq=tq, tk=tk, scale=1.0 / math.sqrt(dim)),
        grid=(batch * heads, nq),
        in_specs=[
            pl.BlockSpec((tq, dim), lambda bh, i: ((bh // heads) * nq + i, q_col + bh % heads)),
            pl.BlockSpec((seq, dim), lambda bh, i: (bh // heads, k_col + bh % heads)),
            pl.BlockSpec((seq, dim), lambda bh, i: (bh // heads, v_col + bh % heads)),
        ],
        out_specs=pl.BlockSpec((tq, dim), lambda bh, i: ((bh // heads) * nq + i, bh % heads)),
        out_shape=jax.ShapeDtypeStruct((batch * seq, heads * dim), BF16),
        scratch_shapes=[pltpu.VMEM((tq, 1), F32), pltpu.VMEM((tq, dim), F32), pltpu.VMEM((tq, tk), F32),
                        pltpu.VMEM((tq, tk), F32), pltpu.VMEM((tq, dim), BF16)],
        compiler_params=_cparams(("arbitrary", "arbitrary")),
        name="stick_breaking",
    )(p, p, p)


FLASH_ROW_BLOCK = 128


def _flash_kernel(q_ref, k_ref, v_ref, o_ref, m_ref, acc_ref, sa_ref, sb_ref, p_ref, alpha_ref, *, qt, tk, scale, window):
    i = pl.program_id(1)
    rb = q_ref.shape[0]
    dv = v_ref.shape[1]
    nch = tk // LANES
    q0 = i * qt
    row = lax.broadcasted_iota(jnp.int32, (rb, 1), 0)
    tok = q0 + (row % qt if rb != qt else row)
    col = lax.broadcasted_iota(jnp.int32, (1, LANES), 1)
    hi_blk = (q0 + qt - 1) // tk
    lo_blk = 0 if window is None else jnp.maximum(q0 - window + 1, 0) // tk
    m_ref[...] = jnp.full(m_ref.shape, NEG, F32)
    acc_ref[...] = jnp.zeros(acc_ref.shape, F32)
    ones = jnp.ones((tk, LANES), BF16)

    def tile_start(j):
        return pl.multiple_of(jnp.minimum(j, hi_blk) * tk, tk)

    def scores(j, s_ref):
        s_ref[...] = _dot_nt(q_ref[...], k_ref[pl.ds(tile_start(j), tk), :])

    def update(j, s_ref, masked):
        v_aug = jnp.concatenate([v_ref[pl.ds(tile_start(j), tk), :], ones], axis=1)
        for r0 in range(0, rb, FLASH_ROW_BLOCK):
            rs = slice(r0, r0 + FLASH_ROW_BLOCK)
            chunks = [s_ref[rs, c * LANES:(c + 1) * LANES] * (scale * LOG2E) for c in range(nch)]
            if masked:
                masks = []
                for c in range(nch):
                    kpos = j * tk + c * LANES + col
                    mk = kpos <= tok[rs]
                    if window is not None:
                        mk = mk & (kpos > tok[rs] - window)
                    masks.append(mk)
                chunks = [jnp.where(mk, ch, NEG) for mk, ch in zip(masks, chunks)]
            cmax = chunks[0]
            for ch in chunks[1:]:
                cmax = jnp.maximum(cmax, ch)
            m_prev = m_ref[rs, :]
            m_new = jnp.maximum(m_prev, jnp.max(cmax, axis=-1, keepdims=True))
            ps = [jnp.exp2(ch - m_new) for ch in chunks]
            if masked:
                ps = [jnp.where(mk, p, 0.0) for mk, p in zip(masks, ps)]
            p_ref[rs, :] = jnp.concatenate(ps, axis=1).astype(BF16)
            alpha_ref[rs, :] = jnp.exp2(m_prev - m_new)
            m_ref[rs, :] = m_new
        pv = _dot(p_ref[...], v_aug)
        alpha = alpha_ref[...]
        for c in range((dv + LANES) // LANES):
            sl = slice(c * LANES, (c + 1) * LANES)
            acc_ref[:, sl] = alpha * acc_ref[:, sl] + pv[:, sl]

    scores(lo_blk, sa_ref)

    def pair(j, masked):
        scores(j + 1, sb_ref)
        update(j, sa_ref, masked)
        scores(j + 2, sa_ref)
        update(j + 1, sb_ref, masked)

    n_free = 0 if window is not None else ((q0 + 1) // tk) // 2

    def free_body(n, carry):
        pair(lo_blk + 2 * n, False)
        return carry

    def masked_body(n, carry):
        pair(lo_blk + 2 * n, True)
        return carry

    if window is None:
        lax.fori_loop(0, n_free, free_body, 0)
    lax.fori_loop(n_free, (hi_blk - lo_blk + 2) // 2, masked_body, 0)
    o_ref[...] = (acc_ref[:, :dv] / jnp.maximum(acc_ref[:, dv:], TINY)).astype(o_ref.dtype)


def flash_attn(q, k, v, *, qt, tk, scale, window=None, kv_rep=1, out_dtype=BF16):
    BH, nQ, rb, dq = q.shape
    S = k.shape[1]
    dv = v.shape[2]
    assert dv == LANES and tk % LANES == 0
    return pl.pallas_call(
        functools.partial(_flash_kernel, qt=qt, tk=tk, scale=scale, window=window),
        grid=(BH, nQ),
        in_specs=[
            pl.BlockSpec((None, None, rb, dq), lambda bh, i: (bh, i, 0, 0)),
            pl.BlockSpec((None, S, dq), lambda bh, i: (bh // kv_rep, 0, 0)),
            pl.BlockSpec((None, S, dv), lambda bh, i: (bh // kv_rep, 0, 0)),
        ],
        out_specs=pl.BlockSpec((None, None, rb, dv), lambda bh, i: (bh, i, 0, 0)),
        out_shape=jax.ShapeDtypeStruct((BH, nQ, rb, dv), out_dtype),
        scratch_shapes=[pltpu.VMEM((rb, LANES), F32), pltpu.VMEM((rb, dv + LANES), F32),
                        pltpu.VMEM((rb, tk), F32), pltpu.VMEM((rb, tk), F32),
                        pltpu.VMEM((rb, tk), BF16), pltpu.VMEM((rb, LANES), F32)],
        compiler_params=_cparams(("arbitrary", "arbitrary")),
        name="flash_attn",
    )(q, k, v)


def _rope_tables(pos, rot_dim, period):
    half = rot_dim // 2
    inv = ROPE_THETA ** (-jnp.arange(half, dtype=F32) / half)
    ang = pos.astype(F32)[..., None] * inv
    cos, sin = jnp.cos(ang), jnp.sin(ang)
    rest = period - rot_dim
    shp = cos.shape[:-1]
    c = jnp.concatenate([cos, cos, jnp.ones(shp + (rest,), F32)], -1)
    sn = jnp.concatenate([-sin, jnp.zeros(shp + (half + rest,), F32)], -1)
    sp = jnp.concatenate([jnp.zeros(shp + (half,), F32), sin, jnp.zeros(shp + (rest,), F32)], -1)
    rep = LANES // period
    return tuple(jnp.tile(t, (1, 1, rep)) for t in (c, sn, sp))


def _rope_lanes(x, c, sn, sp, half):
    return x * c + pltpu.roll(x, LANES - half, 1) * sn + pltpu.roll(x, half, 1) * sp


def _rope_kernel(x_ref, c_ref, sn_ref, sp_ref, o_ref, *, half, n_heads, stacked):
    c, sn, sp = c_ref[...], sn_ref[...], sp_ref[...]
    for h in range(n_heads):
        y = _rope_lanes(x_ref[:, h * LANES:(h + 1) * LANES], c, sn, sp, half).astype(o_ref.dtype)
        if stacked:
            o_ref[h] = y
        else:
            o_ref[:, h * LANES:(h + 1) * LANES] = y


def rope_cols(p, tables, *, seq, col0, n_heads, half, tt, stacked_groups=None):
    T = p.shape[0]
    nb = seq // tt
    width = n_heads * LANES
    tspec = pl.BlockSpec((None, tt, LANES), lambda i, g: (i // nb, i % nb, 0))
    if stacked_groups is None:
        grid = (T // tt, 1)
        out_spec = pl.BlockSpec((tt, width), lambda i, g: (i, 0))
        out_shape = jax.ShapeDtypeStruct((T, width), BF16)
    else:
        grid = (T // tt, stacked_groups)
        out_spec = pl.BlockSpec((None, None, n_heads, tt, LANES), lambda i, g: (i, g, 0, 0, 0))
        out_shape = jax.ShapeDtypeStruct((T // tt, stacked_groups, n_heads, tt, LANES), BF16)
    return pl.pallas_call(
        functools.partial(_rope_kernel, half=half, n_heads=n_heads, stacked=stacked_groups is not None),
        grid=grid,
        in_specs=[pl.BlockSpec((tt, width), lambda i, g: (i, col0 + g)), tspec, tspec, tspec],
        out_specs=out_spec,
        out_shape=out_shape,
        compiler_params=_cparams(("arbitrary", "arbitrary")),
        name="rope",
    )(p, *tables)


def _compress_kernel(t_ref, pe_ref, w1_ref, w2_ref, o_ref):
    t = t_ref[...]
    half = t.shape[1]
    w1a = w1_ref[:half, :]
    w1b = w1_ref[half:, :]
    pe = pe_ref[...]
    bias = _dot3(jnp.broadcast_to(pe[:, :half], (8, half)), w1a.astype(F32)) + _dot3(
        jnp.broadcast_to(pe[:, half:], (8, half)), w1b.astype(F32))
    a = _dot(t, w1a)
    b = _dot(t, w1b)
    n = a.shape[0]
    hid = a + pltpu.roll(b, n - 1, 0) + bias[0:1, :]
    o_ref[...] = _dot(_gelu_tanh(hid).astype(BF16), w2_ref[...]).astype(o_ref.dtype)


def nsa_compress(t2, pe, w1, w2):
    B, G, M, K = t2.shape
    d = w2.shape[1]
    return pl.pallas_call(
        _compress_kernel,
        grid=(B, G),
        in_specs=[
            pl.BlockSpec((None, None, M, K), lambda b, g: (b, g, 0, 0)),
            pl.BlockSpec((1, 2 * K), lambda b, g: (0, 0)),
            pl.BlockSpec((2 * K, w1.shape[1]), lambda b, g: (0, 0)),
            pl.BlockSpec(w2.shape, lambda b, g: (0, 0)),
        ],
        out_specs=pl.BlockSpec((None, None, M, d), lambda b, g: (b, g, 0, 0)),
        out_shape=jax.ShapeDtypeStruct((B, G, M, d), BF16),
        compiler_params=_cparams(("arbitrary", "arbitrary")),
        name="nsa_compress",
    )(t2, pe.reshape(1, 2 * K), w1.astype(BF16), w2.astype(BF16))


CMP_Q_BLOCKS = 8


def _cmp_select_kernel(q_ref, kc_ref, vc_ref, ov_ref, oc_ref, bias_ref, *, nblk, **kw):
    for b in range(nblk):
        _cmp_select_block(pl.program_id(2) * nblk + b, q_ref.at[b], kc_ref, vc_ref, ov_ref, oc_ref.at[b], bias_ref.at[b], **kw)


def _cmp_select_block(i, q_ref, kc_ref, vc_ref, ov_ref, oc_ref, bias_ref, *, qt, n_cmp, n_slc, slc_k, scale):
    rb = q_ref.shape[0]
    R = rb // qt
    ncp = kc_ref.shape[0]
    q = q_ref[...]
    tok = i * qt + lax.broadcasted_iota(jnp.int32, (rb, 1), 0) % qt
    n_id = lax.broadcasted_iota(jnp.int32, (1, ncp), 1)
    valid = ((n_id * CMP_STRIDE + (CMP_BLK - 1)) <= tok) & (n_id < n_cmp)
    s = _dot_nt(q, kc_ref[...]) * scale
    m = jnp.max(jnp.where(valid, s, NEG), axis=-1, keepdims=True)
    e = jnp.where(valid, jnp.exp(s - m), 0.0)
    p = e / jnp.maximum(jnp.sum(e, axis=-1, keepdims=True), TINY)
    oc_ref[...] = _dot(p.astype(BF16), vc_ref[...]).astype(oc_ref.dtype)
    psum = p[0:qt]
    for r in range(1, R):
        psum = psum + p[r * qt:(r + 1) * qt]
    hi, lo = _split(psum)
    ov = ov_ref[...]
    imp = _dot(hi, ov) + _dot(lo, ov)
    t1 = i * qt + lax.broadcasted_iota(jnp.int32, (qt, 1), 0)
    blk = lax.broadcasted_iota(jnp.int32, (1, LANES), 1)
    cur = t1 // SLC_BLK
    forced = (blk == 0) | (blk == cur) | (blk == cur - 1)
    ok = blk * SLC_BLK <= t1
    score = jnp.where(forced, BIG, jnp.where(ok, imp, -BIG))
    score = jnp.where(blk < n_slc, score, -jnp.inf)
    bias = jnp.full((qt, LANES), -BIG, F32)
    for _ in range(slc_k):
        mx = jnp.max(score, axis=-1, keepdims=True)
        idx = jnp.min(jnp.where(score == mx, blk, LANES), axis=-1, keepdims=True)
        hit = blk == idx
        bias = jnp.where(hit, 0.0, bias)
        score = jnp.where(hit, -jnp.inf, score)
    bias_ref[...] = bias.astype(bias_ref.dtype)


def nsa_cmp_select(q_st, kc_c, vc_c, overlap, *, qt, n_cmp, n_slc, slc_k, scale):
    B, G, nQ, rb, d = q_st.shape
    ncp = kc_c.shape[2]
    nblk = CMP_Q_BLOCKS if nQ % CMP_Q_BLOCKS == 0 else 1
    return pl.pallas_call(
        functools.partial(_cmp_select_kernel, nblk=nblk, qt=qt, n_cmp=n_cmp, n_slc=n_slc, slc_k=slc_k, scale=scale),
        grid=(B, G, nQ // nblk),
        in_specs=[
            pl.BlockSpec((None, None, nblk, rb, d), lambda b, g, i: (b, g, i, 0, 0)),
            pl.BlockSpec((None, None, ncp, d), lambda b, g, i: (b, g, 0, 0)),
            pl.BlockSpec((None, None, ncp, d), lambda b, g, i: (b, g, 0, 0)),
            pl.BlockSpec((ncp, LANES), lambda b, g, i: (0, 0)),
        ],
        out_specs=[
            pl.BlockSpec((None, None, nblk, rb, d), lambda b, g, i: (b, g, i, 0, 0)),
            pl.BlockSpec((None, None, nblk, qt, LANES), lambda b, g, i: (b, g, i, 0, 0)),
        ],
        out_shape=[
            jax.ShapeDtypeStruct((B, G, nQ, rb, d), BF16),
            jax.ShapeDtypeStruct((B, G, nQ, qt, LANES), BF16),
        ],
        compiler_params=_cparams(("arbitrary", "arbitrary", "arbitrary")),
        name="nsa_cmp_select",
    )(q_st, kc_c, vc_c, overlap)


def _gate_combine_kernel(oc_ref, os_ref, ow_ref, gl_ref, o_ref, *, R):
    gl = gl_ref[...]
    gs = jax.nn.sigmoid(gl)
    g = pl.program_id(1)
    for r in range(R):
        outs = []
        acc = None
        for br, ref in enumerate((oc_ref, os_ref, ow_ref)):
            lane = (g * R + r) * 3 + br
            sel = lax.broadcasted_iota(jnp.int32, gs.shape, 1) == lane
            gv = jnp.sum(jnp.where(sel, gs, 0.0), axis=-1, keepdims=True)
            term = gv * ref[r].astype(F32)
            acc = term if acc is None else acc + term
        o_ref[:, r * LANES:(r + 1) * LANES] = acc.astype(o_ref.dtype)


def nsa_gate_combine(oc, osl, ow, gl, *, R):
    nT, G, _, tt, d = oc.shape
    T = nT * tt
    ospec = pl.BlockSpec((None, None, R, tt, d), lambda i, g: (i, g, 0, 0, 0))
    return pl.pallas_call(
        functools.partial(_gate_combine_kernel, R=R),
        grid=(nT, G),
        in_specs=[ospec, ospec, ospec, pl.BlockSpec((tt, LANES), lambda i, g: (i, 0))],
        out_specs=pl.BlockSpec((tt, R * d), lambda i, g: (i, g)),
        out_shape=jax.ShapeDtypeStruct((T, G * R * d), BF16),
        compiler_params=_cparams(("arbitrary", "arbitrary")),
        name="nsa_gate_combine",
    )(oc, osl, ow, gl)


def _topk_rows(s, k):
    n = s.shape[0]
    iota = lax.broadcasted_iota(jnp.int32, s.shape, 0)
    vals, idxs = [], []
    for _ in range(k):
        m = jnp.max(s, axis=0, keepdims=True)
        idx = jnp.min(jnp.where(s == m, iota, n), axis=0, keepdims=True)
        vals.append(m)
        idxs.append(idx)
        s = jnp.where(iota == idx, -jnp.inf, s)
    return jnp.concatenate(vals, axis=0), jnp.concatenate(idxs, axis=0)


def _peer_route_kernel(q_ref, k1_ref, k2_ref, e_ref, g_ref, *, topk, n_keys):
    half = k1_ref.shape[1]
    s1 = _dot3_nt(k1_ref[...], q_ref[:, :half])
    s2 = _dot3_nt(k2_ref[...], q_ref[:, half:])
    v1, i1 = _topk_rows(s1, topk)
    v2, i2 = _topk_rows(s2, topk)
    assert topk == 2 * SUBLANES
    tb = s1.shape[1]
    sub = lax.broadcasted_iota(jnp.int32, (SUBLANES, tb), 0)
    cand, cidx, cpos = [], [], []
    for a in range(SUBLANES):
        for m in range(2 if a == 0 else 1):
            bs = slice(m * SUBLANES, (m + 1) * SUBLANES)
            cand.append(v1[a:a + 1, :] + v2[bs, :])
            cidx.append(i1[a:a + 1, :] * n_keys + i2[bs, :])
            cpos.append(a * topk + m * SUBLANES + sub)
    cand.append(v1[SUBLANES:, :] + v2[0:1, :])
    cidx.append(i1[SUBLANES:, :] * n_keys + i2[0:1, :])
    cpos.append((SUBLANES + sub) * topk)
    cand = jnp.concatenate(cand, axis=0)
    cidx = jnp.concatenate(cidx, axis=0)
    cpos = jnp.concatenate(cpos, axis=0)
    tops, exs = [], []
    for _ in range(topk):
        m = jnp.max(cand, axis=0, keepdims=True)
        p = jnp.min(jnp.where(cand == m, cpos, topk * topk), axis=0, keepdims=True)
        hit = cpos == p
        tops.append(m)
        exs.append(jnp.sum(jnp.where(hit, cidx, 0), axis=0, keepdims=True))
        cand = jnp.where(hit, -jnp.inf, cand)
    top = jnp.concatenate(tops, axis=0)
    w = jnp.exp(top - top[0:1])
    e_ref[...] = jnp.concatenate(exs, axis=0)
    g_ref[...] = w / jnp.sum(w, axis=0, keepdims=True)


def peer_route(q, k1, k2, *, heads, topk, tb=1024):
    T, N = q.shape
    n_keys, half = k1.shape
    hk = heads * topk
    tb = min(tb, T)
    return pl.pallas_call(
        functools.partial(_peer_route_kernel, topk=topk, n_keys=n_keys),
        grid=(T // tb, heads),
        in_specs=[
            pl.BlockSpec((tb, 2 * half), lambda i, h: (i, h)),
            pl.BlockSpec((n_keys, half), lambda i, h: (0, 0)),
            pl.BlockSpec((n_keys, half), lambda i, h: (0, 0)),
        ],
        out_specs=[pl.BlockSpec((topk, tb), lambda i, h: (h, i)), pl.BlockSpec((topk, tb), lambda i, h: (h, i))],
        out_shape=[jax.ShapeDtypeStruct((hk, T), jnp.int32), jax.ShapeDtypeStruct((hk, T), F32)],
        compiler_params=_cparams(("arbitrary", "arbitrary")),
        name="peer_route",
    )(q, k1, k2)


def _pack_kernel(u_ref, v_ref, o_ref):
    ub = pltpu.bitcast(u_ref[...].astype(BF16).astype(F32), jnp.uint32) >> 16
    vb = pltpu.bitcast(v_ref[...].astype(BF16).astype(F32), jnp.uint32) & jnp.uint32(0xFFFF0000)
    o_ref[...] = vb | ub


def peer_pack(u, v, layer, te=512):
    _, E, D = u.shape
    in_spec = pl.BlockSpec((None, te, D), lambda i: (layer, i, 0))
    spec = pl.BlockSpec((te, D), lambda i: (i, 0))
    return pl.pallas_call(
        _pack_kernel,
        grid=(E // te,),
        in_specs=[in_spec, in_spec],
        out_specs=spec,
        out_shape=jax.ShapeDtypeStruct((E, D), jnp.uint32),
        compiler_params=_cparams(("arbitrary",)),
        name="peer_pack",
    )(u, v)


PEER_SLOTS = 3


def _peer_mix_kernel(idx0_ref, idx1_ref, idx2_ref, h_ref, g_ref, x_ref, gate_ref, tbl_ref, o_ref, buf_ref, sem_ref, *, tb, nk):
    i = pl.program_id(0)
    n = pl.num_programs(0)
    rows = tb * nk
    nlt = buf_ref.shape[1]
    slot = i % PEER_SLOTS
    nxt = (i + 2) % PEER_SLOTS
    hi_mask = jnp.uint32(0xFFFF0000)

    def row_copy(src_idx_ref, r, dst_slot):
        return pltpu.make_async_copy(tbl_ref.at[src_idx_ref[0, r]], buf_ref.at[dst_slot, :, r, :], sem_ref.at[dst_slot])

    def slot_copy(s):
        return pltpu.make_async_copy(buf_ref.at[s], buf_ref.at[s], sem_ref.at[s])

    @pl.when(i == 0)
    def _():
        def one(r, c):
            row_copy(idx0_ref, r, 0).start()
            row_copy(idx1_ref, r, 1).start()
            return c
        lax.fori_loop(0, rows, one, 0, unroll=8)

    slot_copy(slot).wait()

    ys = []
    for t in range(tb):
        hb = [jnp.broadcast_to(h_ref[t:t + 1, lt * LANES:(lt + 1) * LANES], (SUBLANES, LANES)) for lt in range(nlt)]
        yacc = [None] * nlt
        for c in range(nk // SUBLANES):
            r0 = t * nk + c * SUBLANES
            ws = [buf_ref[slot, lt, r0:r0 + SUBLANES, :] for lt in range(nlt)]
            s = None
            for lt in range(nlt):
                term = pltpu.bitcast(ws[lt] << 16, F32) * hb[lt]
                s = term if s is None else s + term
            act = jnp.sum(s, axis=-1, keepdims=True)
            a = g_ref[c * SUBLANES:(c + 1) * SUBLANES, t:t + 1] * _gelu_tanh(act)
            for lt in range(nlt):
                yv = a * pltpu.bitcast(ws[lt] & hi_mask, F32)
                yacc[lt] = yv if yacc[lt] is None else yacc[lt] + yv
            for r in range(r0, r0 + SUBLANES):
                row_copy(idx2_ref, r, nxt).start(priority=r % 2)
        ys.append(jnp.concatenate([jnp.sum(ya, axis=0, keepdims=True) for ya in yacc], axis=1))
    o_ref[...] = x_ref[...] + gate_ref[...] * jnp.concatenate(ys, axis=0)

    @pl.when(i == n - 1)
    def _():
        slot_copy((i + 1) % PEER_SLOTS).wait()
        slot_copy(nxt).wait()


def peer_mix(experts_t, g_t, h, x, gate, table, *, seq, tb=8):
    T, D = h.shape
    nk = experts_t.shape[0]
    B = gate.shape[0]
    nb = seq // tb
    nblk = T // tb
    idx = experts_t.T.reshape(nblk, 1, tb * nk)
    g = g_t.reshape(nk, nblk, tb).transpose(1, 0, 2)
    smem_spec = lambda f: pl.BlockSpec((None, 1, tb * nk), f, memory_space=pltpu.SMEM)
    return pl.pallas_call(
        functools.partial(_peer_mix_kernel, tb=tb, nk=nk),
        grid=(nblk,),
        in_specs=[
            smem_spec(lambda i: (i, 0, 0)),
            smem_spec(lambda i: (jnp.minimum(i + 1, nblk - 1), 0, 0)),
            smem_spec(lambda i: (jnp.minimum(i + 2, nblk - 1), 0, 0)),
            pl.BlockSpec((tb, D), lambda i: (i, 0)),
            pl.BlockSpec((None, nk, tb), lambda i: (i, 0, 0)),
            pl.BlockSpec((tb, D), lambda i: (i, 0)),
            pl.BlockSpec((None, 1, D), lambda i: (i // nb, 0, 0)),
            pl.BlockSpec(memory_space=pl.ANY),
        ],
        out_specs=pl.BlockSpec((tb, D), lambda i: (i, 0)),
        out_shape=jax.ShapeDtypeStruct((T, D), F32),
        scratch_shapes=[pltpu.VMEM((PEER_SLOTS, D // LANES, tb * nk, LANES), jnp.uint32),
                        pltpu.SemaphoreType.DMA((PEER_SLOTS,))],
        compiler_params=_cparams(("arbitrary",)),
        name="peer_mix",
    )(idx, idx, idx, h, g, x, gate.reshape(B, 1, D), table)


def _rmsnorm_kernel(x_ref, g_ref, o_ref):
    x = x_ref[...]
    o_ref[...] = (x * lax.rsqrt(jnp.mean(x * x, axis=-1, keepdims=True) + EPS)) * g_ref[...]


def rmsnorm_rows(x, g, tm=512):
    T, D = x.shape
    return pl.pallas_call(
        _rmsnorm_kernel,
        grid=(T // tm,),
        in_specs=[pl.BlockSpec((tm, D), lambda i: (i, 0)), pl.BlockSpec((1, D), lambda i: (0, 0))],
        out_specs=pl.BlockSpec((tm, D), lambda i: (i, 0)),
        out_shape=jax.ShapeDtypeStruct((T, D), F32),
        compiler_params=_cparams(("arbitrary",)),
        name="final_rmsnorm",
    )(x, g.reshape(1, D))


def _split3(m):
    d = m.shape[-1] // 3
    return m[:, :d], m[:, d:2 * d], m[:, 2 * d:]


def sb_mla_layer(x, mod, pos, g_norm, w_in, q_norm, w_uq, kv_norm, w_ukv, w_out, *, batch, seq):
    T, D = x.shape
    shift, scale, gate = _split3(mod)
    sbw = SB_HEADS * SB_DIM
    n_in = w_in.shape[1]
    n_pad = -(-n_in // 512) * 512
    p = norm_matmul(x, g_norm, scale, shift, _pad_cols(w_in, n_pad).astype(BF16), seq=seq, tm=512, tn=512)
    o_a = stick_breaking_attn(p, batch=batch, seq=seq, heads=SB_HEADS, dim=SB_DIM,
                              q_col=0, k_col=SB_HEADS, v_col=2 * SB_HEADS)
    c_q = p[:, 3 * sbw:3 * sbw + MLA_Q_RANK]
    c_kv = p[:, 3 * sbw + MLA_Q_RANK:3 * sbw + MLA_Q_RANK + MLA_KV_RANK]
    k_r = p[:, 3 * sbw + MLA_Q_RANK + MLA_KV_RANK:n_in]
    zq = jnp.zeros((batch, MLA_Q_RANK), F32)
    zkv = jnp.zeros((batch, MLA_KV_RANK), F32)
    dqk = MLA_NOPE + MLA_ROPE
    wq = w_uq.reshape(MLA_Q_RANK, MLA_HEADS, dqk)
    wq = jnp.concatenate([wq[:, :, :MLA_NOPE].reshape(MLA_Q_RANK, -1), wq[:, :, MLA_NOPE:].reshape(MLA_Q_RANK, -1)], 1)
    wkv = w_ukv.reshape(MLA_KV_RANK, MLA_HEADS, MLA_NOPE + MLA_V)
    wkv = jnp.concatenate([wkv[:, :, :MLA_NOPE].reshape(MLA_KV_RANK, -1), wkv[:, :, MLA_NOPE:].reshape(MLA_KV_RANK, -1)], 1)
    qf = norm_matmul(c_q, q_norm, zq, zq, wq.astype(BF16), seq=seq, tm=512, tn=512)
    kvf = norm_matmul(c_kv, kv_norm, zkv, zkv, wkv.astype(BF16), seq=seq, tm=512, tn=512, out_dtype=BF16)
    tables = _rope_tables(pos, MLA_ROPE, MLA_ROPE)
    nope_w = MLA_HEADS * MLA_NOPE
    q_rope = rope_cols(qf, tables, seq=seq, col0=nope_w // (MLA_HEADS * MLA_ROPE), n_heads=MLA_HEADS * MLA_ROPE // LANES,
                       half=MLA_ROPE // 2, tt=512)
    kr_pad = jnp.pad(k_r, ((0, 0), (0, LANES - MLA_ROPE)))
    k_rope = rope_cols(kr_pad, tables, seq=seq, col0=0, n_heads=1, half=MLA_ROPE // 2, tt=512)[:, :MLA_ROPE]
    H = MLA_HEADS
    q_nope = qf[:, :nope_w].astype(BF16).reshape(batch, seq, H, MLA_NOPE)
    q_cat = jnp.concatenate([q_nope, q_rope.reshape(batch, seq, H, MLA_ROPE)], -1)
    tq = min(1024, seq)
    q_cat = q_cat.transpose(0, 2, 1, 3).reshape(batch * H, seq // tq, tq, dqk)
    k_nope = kvf[:, :nope_w].reshape(batch, seq, H, MLA_NOPE)
    k_cat = jnp.concatenate([k_nope, jnp.broadcast_to(k_rope.reshape(batch, seq, 1, MLA_ROPE), (batch, seq, H, MLA_ROPE))], -1)
    k_cat = k_cat.transpose(0, 2, 1, 3).reshape(batch * H, seq, dqk)
    v = kvf[:, nope_w:].reshape(batch, seq, H, MLA_V).transpose(0, 2, 1, 3).reshape(batch * H, seq, MLA_V)
    o_b = flash_attn(q_cat, k_cat, v, qt=tq, tk=256, scale=1.0 / math.sqrt(dqk))
    o_b = o_b.reshape(batch, H, seq, MLA_V).transpose(0, 2, 1, 3).reshape(T, H * MLA_V)
    wo = w_out.astype(BF16)
    return matmul_residual([o_a, o_b], [wo[:sbw], wo[sbw:]], x, gate, seq=seq, tm=512, tn=min(512, D))


def nsa_layer(x, mod, pos, g_norm, w_in, pe_k, pe_v, w1_k, w2_k, w1_v, w2_v, w_out, *, batch, seq):
    T, D = x.shape
    shift, scale, gate = _split3(mod)
    G, R, d = NSA_GROUPS, NSA_HEADS // NSA_GROUPS, NSA_DIM
    H = NSA_HEADS
    kvw = G * d
    n_in = w_in.shape[1]
    n_pad = -(-n_in // 768) * 768
    p = norm_matmul(x, g_norm, scale, shift, _pad_cols(w_in, n_pad).astype(BF16), seq=seq, tm=512, tn=768)
    tables = _rope_tables(pos, ROT_DIM, d)
    half = ROT_DIM // 2
    qt = QBLK
    nQ = seq // qt
    q_st = rope_cols(p, tables, seq=seq, col0=0, n_heads=R, half=half, tt=qt, stacked_groups=G)
    q_st = q_st.reshape(batch, nQ, G, R * qt, d).transpose(0, 2, 1, 3, 4)
    base = H * d // kvw
    kc = rope_cols(p, tables, seq=seq, col0=base + 0, n_heads=G, half=half, tt=512)
    ks = rope_cols(p, tables, seq=seq, col0=base + 2, n_heads=G, half=half, tt=512)
    kw = rope_cols(p, tables, seq=seq, col0=base + 4, n_heads=G, half=half, tt=512)
    off = H * d
    vc = p[:, off + kvw:off + 2 * kvw].astype(BF16)
    vs = p[:, off + 3 * kvw:off + 4 * kvw].astype(BF16)
    vw = p[:, off + 5 * kvw:off + 6 * kvw].astype(BF16)
    gl = jnp.pad(p[:, off + 6 * kvw:n_in], ((0, 0), (0, LANES - 3 * H)))

    def per_group(a):
        return a.reshape(batch, seq, G, d).transpose(0, 2, 1, 3)

    M = seq // CMP_STRIDE
    n_cmp = (seq - CMP_BLK) // CMP_STRIDE + 1
    ncp = -(-M // LANES) * LANES
    def chunks(a):
        c = per_group(a).reshape(batch, G, M, CMP_STRIDE * d)
        return jnp.pad(c, ((0, 0), (0, 0), (0, ncp - M), (0, 0)))
    kc_c = nsa_compress(chunks(kc), pe_k, w1_k, w2_k)
    vc_c = nsa_compress(chunks(vc), pe_v, w1_v, w2_v)
    n_slc = seq // SLC_BLK
    slc_k = min(SLC_TOPK, n_slc)
    c_s = np.arange(ncp) * CMP_STRIDE
    s_s = np.arange(LANES) * SLC_BLK
    ovl = np.clip(np.minimum(c_s[:, None] + CMP_BLK, s_s[None, :] + SLC_BLK) - np.maximum(c_s[:, None], s_s[None, :]), 0, None)
    ovl[n_cmp:, :] = 0
    ovl[:, n_slc:] = 0
    scale_a = 1.0 / math.sqrt(d)
    oc, bias = nsa_cmp_select(q_st, kc_c, vc_c, jnp.asarray(ovl, BF16), qt=qt, n_cmp=n_cmp, n_slc=n_slc,
                              slc_k=slc_k, scale=scale_a)
    q_aug = jnp.concatenate([q_st, jnp.tile(bias, (1, 1, 1, R, 1))], -1).reshape(batch * G, nQ, R * qt, 2 * d)
    onehot = (np.arange(seq)[:, None] // SLC_BLK == np.arange(LANES)[None, :]).astype(np.float32)
    ks_g = per_group(ks)
    k_aug = jnp.concatenate([ks_g, jnp.broadcast_to(jnp.asarray(onehot, BF16), ks_g.shape[:2] + onehot.shape)], -1)
    k_aug = k_aug.reshape(batch * G, seq, 2 * d)
    osl = flash_attn(q_aug, k_aug, per_group(vs).reshape(batch * G, seq, d), qt=qt, tk=256, scale=scale_a)
    q_flat = q_st.reshape(batch * G, nQ, R * qt, d)
    ow = flash_attn(q_flat, per_group(kw).reshape(batch * G, seq, d), per_group(vw).reshape(batch * G, seq, d),
                    qt=qt, tk=128, scale=scale_a, window=WINDOW)

    def unstack(o):
        return o.reshape(batch, G, nQ, R, qt, d).transpose(0, 2, 1, 3, 4, 5).reshape(batch * nQ, G, R, qt, d)

    o = nsa_gate_combine(unstack(oc.reshape(batch * G, nQ, R * qt, d)), unstack(osl), unstack(ow), gl, R=R)
    return matmul_residual([o], [w_out.astype(BF16)], x, gate, seq=seq, tm=512, tn=min(512, D))


def peer_layer(x, mod, g_norm, w_q, k1, k2, u_all, v_all, layer, *, batch, seq):
    shift, scale, gate = _split3(mod)
    q, h = norm_matmul(x, g_norm, scale, shift, w_q.astype(BF16), seq=seq, tm=512, tn=512, emit_h=True)
    experts, g = peer_route(q, k1, k2, heads=PEER_HEADS, topk=PEER_TOPK)
    table = peer_pack(u_all, v_all, layer).reshape(u_all.shape[1], u_all.shape[2] // LANES, LANES)
    return peer_mix(experts, g, h, x, gate, table, seq=seq)


def kernel(x, c, positions, norm_mix, ada_mix_w, ada_mix_b, sbmla_w_in, mla_q_norm, mla_w_uq, mla_kv_norm, mla_w_ukv, sbmla_w_out, nsa_w_in, nsa_pe_k, nsa_pe_v, nsa_w1_k, nsa_w2_k, nsa_w1_v, nsa_w2_v, nsa_w_out, norm_ffn, ada_ffn_w, ada_ffn_b, peer_w_q, peer_k1, peer_k2, peer_u, peer_v, final_norm):
    B, S, D = x.shape
    depth = norm_mix.shape[0]
    mod_mix = modulation_all(c, ada_mix_w, ada_mix_b)
    mod_ffn = modulation_all(c, ada_ffn_w, ada_ffn_b)
    xs = x.reshape(B * S, D)
    for layer in range(depth):
        i = layer // 2
        if layer % 2 == 0:
            xs = sb_mla_layer(xs, mod_mix[layer], positions, norm_mix[layer], sbmla_w_in[i], mla_q_norm[i], mla_w_uq[i],
                              mla_kv_norm[i], mla_w_ukv[i], sbmla_w_out[i], batch=B, seq=S)
        else:
            xs = nsa_layer(xs, mod_mix[layer], positions, norm_mix[layer], nsa_w_in[i], nsa_pe_k[i], nsa_pe_v[i],
                           nsa_w1_k[i], nsa_w2_k[i], nsa_w1_v[i], nsa_w2_v[i], nsa_w_out[i], batch=B, seq=S)
        xs = peer_layer(xs, mod_ffn[layer], norm_ffn[layer], peer_w_q[layer], peer_k1[layer], peer_k2[layer],
                        peer_u, peer_v, layer, batch=B, seq=S)
    return rmsnorm_rows(xs, final_norm).reshape(B, S, D)
```

```python
import functools
import math

import numpy as np
import jax
import jax.numpy as jnp
from jax import lax
from jax.experimental import pallas as pl
from jax.experimental.pallas import tpu as pltpu

F32 = jnp.float32
BF16 = jnp.bfloat16

QBLK = 128
ROPE_THETA = 500000.0
EPS = 1e-6
TINY = 1e-30
BIG = 1e9
NEG = -1e30

SB_HEADS = 8
SB_DIM = 128
MLA_HEADS = 8
MLA_Q_RANK = 512
MLA_KV_RANK = 256
MLA_NOPE = 128
MLA_ROPE = 64
MLA_V = 128
NSA_HEADS = 16
NSA_GROUPS = 2
NSA_DIM = 128
ROT_DIM = NSA_DIM // 4
CMP_BLK = 32
CMP_STRIDE = 16
CMP_HIDDEN = 256
SLC_BLK = 64
SLC_TOPK = 16
WINDOW = 512
PEER_HEADS = 8
PEER_KEYS = 128
PEER_DKEY = 256
PEER_TOPK = 16

LANES = 128
SUBLANES = 8
VMEM_LIMIT = 56 * 1024 * 1024


def _cparams(sem):
    return pltpu.CompilerParams(dimension_semantics=sem, vmem_limit_bytes=VMEM_LIMIT)


def _split(a):
    hi = a.astype(BF16)
    lo = (a - hi.astype(F32)).astype(BF16)
    return hi, lo


def _dot(a, b):
    return jnp.dot(a, b, preferred_element_type=F32)


def _dot_nt(a, b):
    return lax.dot_general(a, b, (((1,), (1,)), ((), ())), preferred_element_type=F32)


def _dot3(a, b):
    ah, al = _split(a)
    bh, bl = _split(b)
    return _dot(ah, bh) + (_dot(ah, bl) + _dot(al, bh))


def _dot3_nt(a, b):
    ah, al = _split(a)
    bh, bl = _split(b)
    return _dot_nt(ah, bh) + (_dot_nt(ah, bl) + _dot_nt(al, bh))


def _gelu_tanh(x):
    return 0.5 * x * (1.0 + jnp.tanh(math.sqrt(2.0 / math.pi) * (x + 0.044715 * (x * x * x))))


def _pad_cols(w, n):
    return jnp.pad(w, ((0, 0), (0, n - w.shape[1])))


def _mod_kernel(c_ref, w_ref, b_ref, o_ref):
    c = c_ref[...]
    s = c * jax.nn.sigmoid(c)
    o_ref[0] = _dot3(s, w_ref[0]) + b_ref[0]


def modulation_all(c, w, b):
    L, D, N = w.shape
    B = c.shape[0]
    rows = 8
    cp = jnp.pad(c, ((0, rows - B), (0, 0)))
    tn = 768 if N % 768 == 0 else N
    out = pl.pallas_call(
        _mod_kernel,
        grid=(L, N // tn),
        in_specs=[
            pl.BlockSpec((rows, D), lambda l, j: (0, 0)),
            pl.BlockSpec((1, D, tn), lambda l, j: (l, 0, j)),
            pl.BlockSpec((1, 1, tn), lambda l, j: (l, 0, j)),
        ],
        out_specs=pl.BlockSpec((1, rows, tn), lambda l, j: (l, 0, j)),
        out_shape=jax.ShapeDtypeStruct((L, rows, N), F32),
        compiler_params=_cparams(("arbitrary", "arbitrary")),
        name="adaln_mod",
    )(cp, w, b.reshape(L, 1, N))
    return out[:, :B]


def _norm_mm_kernel(x_ref, g_ref, sc_ref, sh_ref, w_ref, o_ref, *rest, emit_h):
    if emit_h:
        h_ref, hb_ref = rest
    else:
        (hb_ref,) = rest

    @pl.when(pl.program_id(1) == 0)
    def _():
        x = x_ref[...]
        y = x * lax.rsqrt(jnp.mean(x * x, axis=-1, keepdims=True) + EPS)
        h = (y * g_ref[...]) * (1.0 + sc_ref[...]) + sh_ref[...]
        hb_ref[...] = h.astype(BF16)
        if emit_h:
            h_ref[...] = h

    o_ref[...] = _dot(hb_ref[...], w_ref[...]).astype(o_ref.dtype)


def norm_matmul(x, g, scale, shift, w, *, seq, tm, tn, emit_h=False, out_dtype=F32):
    T, K = x.shape
    N = w.shape[1]
    assert T % tm == 0 and N % tn == 0 and seq % tm == 0
    nb = seq // tm
    B = scale.shape[0]
    out_shape = [jax.ShapeDtypeStruct((T, N), out_dtype)]
    out_specs = [pl.BlockSpec((tm, tn), lambda i, j: (i, j))]
    if emit_h:
        out_shape.append(jax.ShapeDtypeStruct((T, K), F32))
        out_specs.append(pl.BlockSpec((tm, K), lambda i, j: (i, 0)))
    res = pl.pallas_call(
        functools.partial(_norm_mm_kernel, emit_h=emit_h),
        grid=(T // tm, N // tn),
        in_specs=[
            pl.BlockSpec((tm, K), lambda i, j: (i, 0)),
            pl.BlockSpec((1, K), lambda i, j: (0, 0)),
            pl.BlockSpec((None, 1, K), lambda i, j: (i // nb, 0, 0)),
            pl.BlockSpec((None, 1, K), lambda i, j: (i // nb, 0, 0)),
            pl.BlockSpec((K, tn), lambda i, j: (0, j)),
        ],
        out_specs=out_specs,
        out_shape=out_shape,
        scratch_shapes=[pltpu.VMEM((tm, K), BF16)],
        compiler_params=_cparams(("arbitrary", "arbitrary")),
        name="norm_matmul",
    )(x, g.reshape(1, K), scale.reshape(B, 1, K), shift.reshape(B, 1, K), w)
    return res if emit_h else res[0]


def _mm_res_kernel(*refs, n_in):
    a_refs = refs[:n_in]
    w_refs = refs[n_in:2 * n_in]
    x_ref, gate_ref, o_ref = refs[2 * n_in:]
    y = _dot(a_refs[0][...], w_refs[0][...])
    for a_ref, w_ref in zip(a_refs[1:], w_refs[1:]):
        y = y + _dot(a_ref[...], w_ref[...])
    o_ref[...] = x_ref[...] + gate_ref[...] * y


def matmul_residual(a_list, w_list, x, gate, *, seq, tm, tn):
    T, D = x.shape
    B = gate.shape[0]
    nb = seq // tm
    n_in = len(a_list)
    in_specs = [pl.BlockSpec((tm, a.shape[1]), lambda i, j: (i, 0)) for a in a_list]
    in_specs += [pl.BlockSpec((w.shape[0], tn), lambda i, j: (0, j)) for w in w_list]
    in_specs += [
        pl.BlockSpec((tm, tn), lambda i, j: (i, j)),
        pl.BlockSpec((None, 1, tn), lambda i, j: (i // nb, 0, j)),
    ]
    return pl.pallas_call(
        functools.partial(_mm_res_kernel, n_in=n_in),
        grid=(T // tm, D // tn),
        in_specs=in_specs,
        out_specs=pl.BlockSpec((tm, tn), lambda i, j: (i, j)),
        out_shape=jax.ShapeDtypeStruct((T, D), F32),
        compiler_params=_cparams(("arbitrary", "arbitrary")),
        name="matmul_residual",
    )(*a_list, *w_list, x, gate.reshape(B, 1, D))


LOG2E = 1.4426950408889634


def _sb_kernel(q_ref, k_ref, v_ref, o_ref, c_ref, acc_ref, za_ref, zb_ref, qb_ref, *, tq, tk, scale):
    i = pl.program_id(1)
    qb_ref[...] = q_ref[...].astype(BF16)
    t_idx = i * tq + lax.broadcasted_iota(jnp.int32, (tq, 1), 0)
    col = lax.broadcasted_iota(jnp.int32, (1, tk), 1)
    tri = (lax.broadcasted_iota(jnp.int32, (tk, tk), 0) > lax.broadcasted_iota(jnp.int32, (tk, tk), 1)).astype(BF16)
    tri2 = jnp.concatenate([tri, tri], axis=0)
    c_ref[...] = jnp.zeros(c_ref.shape, F32)
    acc_ref[...] = jnp.zeros(acc_ref.shape, F32)
    top = 2 * i + 1

    def tile_start(n):
        return pl.multiple_of(jnp.maximum(top - n, 0) * tk, tk)

    def scores(n, z_ref):
        z_ref[...] = _dot_nt(qb_ref[...], k_ref[pl.ds(tile_start(n), tk), :].astype(BF16))

    def update(n, z_ref, masked):
        v = v_ref[pl.ds(tile_start(n), tk), :].astype(BF16)
        z = z_ref[...] * (scale * LOG2E)
        sp = jnp.maximum(z, 0.0) + jnp.log2(1.0 + jnp.exp2(-jnp.abs(z)))
        if masked:
            strict = ((top - n) * tk + col) < t_idx
            lp = jnp.where(strict, sp, 0.0)
        else:
            lp = sp
        hi, lo = _split(lp)
        suf = _dot(jnp.concatenate([hi, lo], axis=1), tri2)
        c = c_ref[...]
        a = jnp.exp2((z - sp) - (suf + c))
        if masked:
            a = jnp.where(strict, a, 0.0)
        acc_ref[...] += _dot(a.astype(BF16), v)
        c_ref[...] = c + jnp.sum(lp, axis=-1, keepdims=True)

    scores(0, za_ref)
    scores(1, zb_ref)
    update(0, za_ref, True)
    scores(2, za_ref)
    update(1, zb_ref, True)

    def body(m, carry):
        n = 2 * m
        scores(n + 1, zb_ref)
        update(n, za_ref, False)
        scores(n + 2, za_ref)
        update(n + 1, zb_ref, False)
        return carry

    lax.fori_loop(1, i + 1, body, 0)
    o_ref[...] = acc_ref[...].astype(o_ref.dtype)


def stick_breaking_attn(p, *, batch, seq, heads, dim, q_col, k_col, v_col, tq=512):
    nq = seq // tq
    tk = tq // 2
    return pl.pallas_call(
        functools.partial(_sb_kernel, tq=tq, tk=tk, scale=1.0 / math.sqrt(dim)),
        grid=(batch * heads, nq),
        in_specs=[
            pl.BlockSpec((tq, dim), lambda bh, i: ((bh // heads) * nq + i, q_col + bh % heads)),
            pl.BlockSpec((seq, dim), lambda bh, i: (bh // heads, k_col + bh % heads)),
            pl.BlockSpec((seq, dim), lambda bh, i: (bh // heads, v_col + bh % heads)),
        ],
        out_specs=pl.BlockSpec((tq, dim), lambda bh, i: ((bh // heads) * nq + i, bh % heads)),
        out_shape=jax.ShapeDtypeStruct((batch * seq, heads * dim), BF16),
        scratch_shapes=[pltpu.VMEM((tq, 1), F32), pltpu.VMEM((tq, dim), F32), pltpu.VMEM((tq, tk), F32),
                        pltpu.VMEM((tq, tk), F32), pltpu.VMEM((tq, dim), BF16)],
        compiler_params=_cparams(("arbitrary", "arbitrary")),
        name="stick_breaking",
    )(p, p, p)


FLASH_ROW_BLOCK = 128


def _flash_kernel(q_ref, k_ref, v_ref, o_ref, m_ref, acc_ref, sa_ref, sb_ref, p_ref, alpha_ref, *, qt, tk, scale, window):
    i = pl.program_id(1)
    rb = q_ref.shape[0]
    dv = v_ref.shape[1]
    nch = tk // LANES
    q0 = i * qt
    row = lax.broadcasted_iota(jnp.int32, (rb, 1), 0)
    tok = q0 + (row % qt if rb != qt else row)
    col = lax.broadcasted_iota(jnp.int32, (1, LANES), 1)
    hi_blk = (q0 + qt - 1) // tk
    lo_blk = 0 if window is None else jnp.maximum(q0 - window + 1, 0) // tk
    m_ref[...] = jnp.full(m_ref.shape, NEG, F32)
    acc_ref[...] = jnp.zeros(acc_ref.shape, F32)
    ones = jnp.ones((tk, LANES), BF16)

    def tile_start(j):
        return pl.multiple_of(jnp.minimum(j, hi_blk) * tk, tk)

    def scores(j, s_ref):
        s_ref[...] = _dot_nt(q_ref[...], k_ref[pl.ds(tile_start(j), tk), :])

    def update(j, s_ref, masked):
        v_aug = jnp.concatenate([v_ref[pl.ds(tile_start(j), tk), :], ones], axis=1)
        for r0 in range(0, rb, FLASH_ROW_BLOCK):
            rs = slice(r0, r0 + FLASH_ROW_BLOCK)
            chunks = [s_ref[rs, c * LANES:(c + 1) * LANES] * (scale * LOG2E) for c in range(nch)]
            if masked:
                masks = []
                for c in range(nch):
                    kpos = j * tk + c * LANES + col
                    mk = kpos <= tok[rs]
                    if window is not None:
                        mk = mk & (kpos > tok[rs] - window)
                    masks.append(mk)
                chunks = [jnp.where(mk, ch, NEG) for mk, ch in zip(masks, chunks)]
            cmax = chunks[0]
            for ch in chunks[1:]:
                cmax = jnp.maximum(cmax, ch)
            m_prev = m_ref[rs, :]
            m_new = jnp.maximum(m_prev, jnp.max(cmax, axis=-1, keepdims=True))
            ps = [jnp.exp2(ch - m_new) for ch in chunks]
            if masked:
                ps = [jnp.where(mk, p, 0.0) for mk, p in zip(masks, ps)]
            p_ref[rs, :] = jnp.concatenate(ps, axis=1).astype(BF16)
            alpha_ref[rs, :] = jnp.exp2(m_prev - m_new)
            m_ref[rs, :] = m_new
        pv = _dot(p_ref[...], v_aug)
        alpha = alpha_ref[...]
        for c in range((dv + LANES) // LANES):
            sl = slice(c * LANES, (c + 1) * LANES)
            acc_ref[:, sl] = alpha * acc_ref[:, sl] + pv[:, sl]

    scores(lo_blk, sa_ref)

    def pair(j, masked):
        scores(j + 1, sb_ref)
        update(j, sa_ref, masked)
        scores(j + 2, sa_ref)
        update(j + 1, sb_ref, masked)

    n_free = 0 if window is not None else ((q0 + 1) // tk) // 2

    def free_body(n, carry):
        pair(lo_blk + 2 * n, False)
        return carry

    def masked_body(n, carry):
        pair(lo_blk + 2 * n, True)
        return carry

    if window is None:
        lax.fori_loop(0, n_free, free_body, 0)
    lax.fori_loop(n_free, (hi_blk - lo_blk + 2) // 2, masked_body, 0)
    o_ref[...] = (acc_ref[:, :dv] / jnp.maximum(acc_ref[:, dv:], TINY)).astype(o_ref.dtype)


def flash_attn(q, k, v, *, qt, tk, scale, window=None, kv_rep=1, out_dtype=BF16):
    BH, nQ, rb, dq = q.shape
    S = k.shape[1]
    dv = v.shape[2]
    assert dv == LANES and tk % LANES == 0
    return pl.pallas_call(
        functools.partial(_flash_kernel, qt=qt, tk=tk, scale=scale, window=window),
        grid=(BH, nQ),
        in_specs=[
            pl.BlockSpec((None, None, rb, dq), lambda bh, i: (bh, i, 0, 0)),
            pl.BlockSpec((None, S, dq), lambda bh, i: (bh // kv_rep, 0, 0)),
            pl.BlockSpec((None, S, dv), lambda bh, i: (bh // kv_rep, 0, 0)),
        ],
        out_specs=pl.BlockSpec((None, None, rb, dv), lambda bh, i: (bh, i, 0, 0)),
        out_shape=jax.ShapeDtypeStruct((BH, nQ, rb, dv), out_dtype),
        scratch_shapes=[pltpu.VMEM((rb, LANES), F32), pltpu.VMEM((rb, dv + LANES), F32),
                        pltpu.VMEM((rb, tk), F32), pltpu.VMEM((rb, tk), F32),
                        pltpu.VMEM((rb, tk), BF16), pltpu.VMEM((rb, LANES), F32)],
        compiler_params=_cparams(("arbitrary", "arbitrary")),
        name="flash_attn",
    )(q, k, v)


def _rope_tables(pos, rot_dim, period):
    half = rot_dim // 2
    inv = ROPE_THETA ** (-jnp.arange(half, dtype=F32) / half)
    ang = pos.astype(F32)[..., None] * inv
    cos, sin = jnp.cos(ang), jnp.sin(ang)
    rest = period - rot_dim
    shp = cos.shape[:-1]
    c = jnp.concatenate([cos, cos, jnp.ones(shp + (rest,), F32)], -1)
    sn = jnp.concatenate([-sin, jnp.zeros(shp + (half + rest,), F32)], -1)
    sp = jnp.concatenate([jnp.zeros(shp + (half,), F32), sin, jnp.zeros(shp + (rest,), F32)], -1)
    rep = LANES // period
    return tuple(jnp.tile(t, (1, 1, rep)) for t in (c, sn, sp))


def _rope_lanes(x, c, sn, sp, half):
    return x * c + pltpu.roll(x, LANES - half, 1) * sn + pltpu.roll(x, half, 1) * sp


def _rope_kernel(x_ref, c_ref, sn_ref, sp_ref, o_ref, *, half, n_heads, stacked):
    c, sn, sp = c_ref[...], sn_ref[...], sp_ref[...]
    for h in range(n_heads):
        y = _rope_lanes(x_ref[:, h * LANES:(h + 1) * LANES], c, sn, sp, half).astype(o_ref.dtype)
        if stacked:
            o_ref[h] = y
        else:
            o_ref[:, h * LANES:(h + 1) * LANES] = y


def rope_cols(p, tables, *, seq, col0, n_heads, half, tt, stacked_groups=None):
    T = p.shape[0]
    nb = seq // tt
    width = n_heads * LANES
    tspec = pl.BlockSpec((None, tt, LANES), lambda i, g: (i // nb, i % nb, 0))
    if stacked_groups is None:
        grid = (T // tt, 1)
        out_spec = pl.BlockSpec((tt, width), lambda i, g: (i, 0))
        out_shape = jax.ShapeDtypeStruct((T, width), BF16)
    else:
        grid = (T // tt, stacked_groups)
        out_spec = pl.BlockSpec((None, None, n_heads, tt, LANES), lambda i, g: (i, g, 0, 0, 0))
        out_shape = jax.ShapeDtypeStruct((T // tt, stacked_groups, n_heads, tt, LANES), BF16)
    return pl.pallas_call(
        functools.partial(_rope_kernel, half=half, n_heads=n_heads, stacked=stacked_groups is not None),
        grid=grid,
        in_specs=[pl.BlockSpec((tt, width), lambda i, g: (i, col0 + g)), tspec, tspec, tspec],
        out_specs=out_spec,
        out_shape=out_shape,
        compiler_params=_cparams(("arbitrary", "arbitrary")),
        name="rope",
    )(p, *tables)


def _compress_kernel(t_ref, pe_ref, w1_ref, w2_ref, o_ref):
    t = t_ref[...]
    half = t.shape[1]
    w1a = w1_ref[:half, :]
    w1b = w1_ref[half:, :]
    pe = pe_ref[...]
    bias = _dot3(jnp.broadcast_to(pe[:, :half], (8, half)), w1a.astype(F32)) + _dot3(
        jnp.broadcast_to(pe[:, half:], (8, half)), w1b.astype(F32))
    a = _dot(t, w1a)
    b = _dot(t, w1b)
    n = a.shape[0]
    hid = a + pltpu.roll(b, n - 1, 0) + bias[0:1, :]
    o_ref[...] = _dot(_gelu_tanh(hid).astype(BF16), w2_ref[...]).astype(o_ref.dtype)


def nsa_compress(t2, pe, w1, w2):
    B, G, M, K = t2.shape
    d = w2.shape[1]
    return pl.pallas_call(
        _compress_kernel,
        grid=(B, G),
        in_specs=[
            pl.BlockSpec((None, None, M, K), lambda b, g: (b, g, 0, 0)),
            pl.BlockSpec((1, 2 * K), lambda b, g: (0, 0)),
            pl.BlockSpec((2 * K, w1.shape[1]), lambda b, g: (0, 0)),
            pl.BlockSpec(w2.shape, lambda b, g: (0, 0)),
        ],
        out_specs=pl.BlockSpec((None, None, M, d), lambda b, g: (b, g, 0, 0)),
        out_shape=jax.ShapeDtypeStruct((B, G, M, d), BF16),
        compiler_params=_cparams(("arbitrary", "arbitrary")),
        name="nsa_compress",
    )(t2, pe.reshape(1, 2 * K), w1.astype(BF16), w2.astype(BF16))


CMP_Q_BLOCKS = 8


def _cmp_select_kernel(q_ref, kc_ref, vc_ref, ov_ref, oc_ref, bias_ref, *, nblk, **kw):
    for b in range(nblk):
        _cmp_select_block(pl.program_id(2) * nblk + b, q_ref.at[b], kc_ref, vc_ref, ov_ref, oc_ref.at[b], bias_ref.at[b], **kw)


def _cmp_select_block(i, q_ref, kc_ref, vc_ref, ov_ref, oc_ref, bias_ref, *, qt, n_cmp, n_slc, slc_k, scale):
    rb = q_ref.shape[0]
    R = rb // qt
    ncp = kc_ref.shape[0]
    q = q_ref[...]
    tok = i * qt + lax.broadcasted_iota(jnp.int32, (rb, 1), 0) % qt
    n_id = lax.broadcasted_iota(jnp.int32, (1, ncp), 1)
    valid = ((n_id * CMP_STRIDE + (CMP_BLK - 1)) <= tok) & (n_id < n_cmp)
    s = _dot_nt(q, kc_ref[...]) * scale
    m = jnp.max(jnp.where(valid, s, NEG), axis=-1, keepdims=True)
    e = jnp.where(valid, jnp.exp(s - m), 0.0)
    p = e / jnp.maximum(jnp.sum(e, axis=-1, keepdims=True), TINY)
    oc_ref[...] = _dot(p.astype(BF16), vc_ref[...]).astype(oc_ref.dtype)
    psum = p[0:qt]
    for r in range(1, R):
        psum = psum + p[r * qt:(r + 1) * qt]
    hi, lo = _split(psum)
    ov = ov_ref[...]
    imp = _dot(hi, ov) + _dot(lo, ov)
    t1 = i * qt + lax.broadcasted_iota(jnp.int32, (qt, 1), 0)
    blk = lax.broadcasted_iota(jnp.int32, (1, LANES), 1)
    cur = t1 // SLC_BLK
    forced = (blk == 0) | (blk == cur) | (blk == cur - 1)
    ok = blk * SLC_BLK <= t1
    score = jnp.where(forced, BIG, jnp.where(ok, imp, -BIG))
    score = jnp.where(blk < n_slc, score, -jnp.inf)
    bias = jnp.full((qt, LANES), -BIG, F32)
    for _ in range(slc_k):
        mx = jnp.max(score, axis=-1, keepdims=True)
        idx = jnp.min(jnp.where(score == mx, blk, LANES), axis=-1, keepdims=True)
        hit = blk == idx
        bias = jnp.where(hit, 0.0, bias)
        score = jnp.where(hit, -jnp.inf, score)
    bias_ref[...] = bias.astype(bias_ref.dtype)


def nsa_cmp_select(q_st, kc_c, vc_c, overlap, *, qt, n_cmp, n_slc, slc_k, scale):
    B, G, nQ, rb, d = q_st.shape
    ncp = kc_c.shape[2]
    nblk = CMP_Q_BLOCKS if nQ % CMP_Q_BLOCKS == 0 else 1
    return pl.pallas_call(
        functools.partial(_cmp_select_kernel, nblk=nblk, qt=qt, n_cmp=n_cmp, n_slc=n_slc, slc_k=slc_k, scale=scale),
        grid=(B, G, nQ // nblk),
        in_specs=[
            pl.BlockSpec((None, None, nblk, rb, d), lambda b, g, i: (b, g, i, 0, 0)),
            pl.BlockSpec((None, None, ncp, d), lambda b, g, i: (b, g, 0, 0)),
            pl.BlockSpec((None, None, ncp, d), lambda b, g, i: (b, g, 0, 0)),
            pl.BlockSpec((ncp, LANES), lambda b, g, i: (0, 0)),
        ],
        out_specs=[
            pl.BlockSpec((None, None, nblk, rb, d), lambda b, g, i: (b, g, i, 0, 0)),
            pl.BlockSpec((None, None, nblk, qt, LANES), lambda b, g, i: (b, g, i, 0, 0)),
        ],
        out_shape=[
            jax.ShapeDtypeStruct((B, G, nQ, rb, d), BF16),
            jax.ShapeDtypeStruct((B, G, nQ, qt, LANES), BF16),
        ],
        compiler_params=_cparams(("arbitrary", "arbitrary", "arbitrary")),
        name="nsa_cmp_select",
    )(q_st, kc_c, vc_c, overlap)


def _gate_combine_kernel(oc_ref, os_ref, ow_ref, gl_ref, o_ref, *, R):
    gl = gl_ref[...]
    gs = jax.nn.sigmoid(gl)
    g = pl.program_id(1)
    for r in range(R):
        outs = []
        acc = None
        for br, ref in enumerate((oc_ref, os_ref, ow_ref)):
            lane = (g * R + r) * 3 + br
            sel = lax.broadcasted_iota(jnp.int32, gs.shape, 1) == lane
            gv = jnp.sum(jnp.where(sel, gs, 0.0), axis=-1, keepdims=True)
            term = gv * ref[r].astype(F32)
            acc = term if acc is None else acc + term
        o_ref[:, r * LANES:(r + 1) * LANES] = acc.astype(o_ref.dtype)


def nsa_gate_combine(oc, osl, ow, gl, *, R):
    nT, G, _, tt, d = oc.shape
    T = nT * tt
    ospec = pl.BlockSpec((None, None, R, tt, d), lambda i, g: (i, g, 0, 0, 0))
    return pl.pallas_call(
        functools.partial(_gate_combine_kernel, R=R),
        grid=(nT, G),
        in_specs=[ospec, ospec, ospec, pl.BlockSpec((tt, LANES), lambda i, g: (i, 0))],
        out_specs=pl.BlockSpec((tt, R * d), lambda i, g: (i, g)),
        out_shape=jax.ShapeDtypeStruct((T, G * R * d), BF16),
        compiler_params=_cparams(("arbitrary", "arbitrary")),
        name="nsa_gate_combine",
    )(oc, osl, ow, gl)


def _topk_rows(s, k):
    n = s.shape[0]
    iota = lax.broadcasted_iota(jnp.int32, s.shape, 0)
    vals, idxs = [], []
    for _ in range(k):
        m = jnp.max(s, axis=0, keepdims=True)
        idx = jnp.min(jnp.where(s == m, iota, n), axis=0, keepdims=True)
        vals.append(m)
        idxs.append(idx)
        s = jnp.where(iota == idx, -jnp.inf, s)
    return jnp.concatenate(vals, axis=0), jnp.concatenate(idxs, axis=0)


def _peer_route_kernel(q_ref, k1_ref, k2_ref, e_ref, g_ref, *, topk, n_keys):
    half = k1_ref.shape[1]
    s1 = _dot3_nt(k1_ref[...], q_ref[:, :half])
    s2 = _dot3_nt(k2_ref[...], q_ref[:, half:])
    v1, i1 = _topk_rows(s1, topk)
    v2, i2 = _topk_rows(s2, topk)
    assert topk == 2 * SUBLANES
    tb = s1.shape[1]
    sub = lax.broadcasted_iota(jnp.int32, (SUBLANES, tb), 0)
    cand, cidx, cpos = [], [], []
    for a in range(SUBLANES):
        for m in range(2 if a == 0 else 1):
            bs = slice(m * SUBLANES, (m + 1) * SUBLANES)
            cand.append(v1[a:a + 1, :] + v2[bs, :])
            cidx.append(i1[a:a + 1, :] * n_keys + i2[bs, :])
            cpos.append(a * topk + m * SUBLANES + sub)
    cand.append(v1[SUBLANES:, :] + v2[0:1, :])
    cidx.append(i1[SUBLANES:, :] * n_keys + i2[0:1, :])
    cpos.append((SUBLANES + sub) * topk)
    cand = jnp.concatenate(cand, axis=0)
    cidx = jnp.concatenate(cidx, axis=0)
    cpos = jnp.concatenate(cpos, axis=0)
    tops, exs = [], []
    for _ in range(topk):
        m = jnp.max(cand, axis=0, keepdims=True)
        p = jnp.min(jnp.where(cand == m, cpos, topk * topk), axis=0, keepdims=True)
        hit = cpos == p
        tops.append(m)
        exs.append(jnp.sum(jnp.where(hit, cidx, 0), axis=0, keepdims=True))
        cand = jnp.where(hit, -jnp.inf, cand)
    top = jnp.concatenate(tops, axis=0)
    w = jnp.exp(top - top[0:1])
    e_ref[...] = jnp.concatenate(exs, axis=0)
    g_ref[...] = w / jnp.sum(w, axis=0, keepdims=True)


def peer_route(q, k1, k2, *, heads, topk, tb=1024):
    T, N = q.shape
    n_keys, half = k1.shape
    hk = heads * topk
    tb = min(tb, T)
    return pl.pallas_call(
        functools.partial(_peer_route_kernel, topk=topk, n_keys=n_keys),
        grid=(T // tb, heads),
        in_specs=[
            pl.BlockSpec((tb, 2 * half), lambda i, h: (i, h)),
            pl.BlockSpec((n_keys, half), lambda i, h: (0, 0)),
            pl.BlockSpec((n_keys, half), lambda i, h: (0, 0)),
        ],
        out_specs=[pl.BlockSpec((topk, tb), lambda i, h: (h, i)), pl.BlockSpec((topk, tb), lambda i, h: (h, i))],
        out_shape=[jax.ShapeDtypeStruct((hk, T), jnp.int32), jax.ShapeDtypeStruct((hk, T), F32)],
        compiler_params=_cparams(("arbitrary", "arbitrary")),
        name="peer_route",
    )(q, k1, k2)


def _pack_kernel(u_ref, v_ref, o_ref):
    ub = pltpu.bitcast(u_ref[...].astype(BF16).astype(F32), jnp.uint32) >> 16
    vb = pltpu.bitcast(v_ref[...].astype(BF16).astype(F32), jnp.uint32) & jnp.uint32(0xFFFF0000)
    o_ref[...] = vb | ub


def peer_pack(u, v, layer, te=512):
    _, E, D = u.shape
    in_spec = pl.BlockSpec((None, te, D), lambda i: (layer, i, 0))
    spec = pl.BlockSpec((te, D), lambda i: (i, 0))
    return pl.pallas_call(
        _pack_kernel,
        grid=(E // te,),
        in_specs=[in_spec, in_spec],
        out_specs=spec,
        out_shape=jax.ShapeDtypeStruct((E, D), jnp.uint32),
        compiler_params=_cparams(("arbitrary",)),
        name="peer_pack",
    )(u, v)


PEER_SLOTS = 3


def _peer_mix_kernel(idx0_ref, idx1_ref, idx2_ref, h_ref, g_ref, x_ref, gate_ref, fn_ref, tbl_ref, o_ref, buf_ref, sem_ref, *, tb, nk, final_norm):
    i = pl.program_id(0)
    n = pl.num_programs(0)
    rows = tb * nk
    nlt = buf_ref.shape[1]
    slot = i % PEER_SLOTS
    nxt = (i + 2) % PEER_SLOTS
    hi_mask = jnp.uint32(0xFFFF0000)

    def row_copy(src_idx_ref, r, dst_slot):
        return pltpu.make_async_copy(tbl_ref.at[src_idx_ref[0, r]], buf_ref.at[dst_slot, :, r, :], sem_ref.at[dst_slot])

    def slot_copy(s):
        return pltpu.make_async_copy(buf_ref.at[s], buf_ref.at[s], sem_ref.at[s])

    @pl.when(i == 0)
    def _():
        def one(r, c):
            row_copy(idx0_ref, r, 0).start()
            row_copy(idx1_ref, r, 1).start()
            return c
        lax.fori_loop(0, rows, one, 0, unroll=8)

    slot_copy(slot).wait()

    ys = []
    for t in range(tb):
        hb = [jnp.broadcast_to(h_ref[t:t + 1, lt * LANES:(lt + 1) * LANES], (SUBLANES, LANES)) for lt in range(nlt)]
        yacc = [None] * nlt
        for c in range(nk // SUBLANES):
            r0 = t * nk + c * SUBLANES
            ws = [buf_ref[slot, lt, r0:r0 + SUBLANES, :] for lt in range(nlt)]
            s = None
            for lt in range(nlt):
                term = pltpu.bitcast(ws[lt] << 16, F32) * hb[lt]
                s = term if s is None else s + term
            act = jnp.sum(s, axis=-1, keepdims=True)
            a = g_ref[c * SUBLANES:(c + 1) * SUBLANES, t:t + 1] * _gelu_tanh(act)
            for lt in range(nlt):
                yv = a * pltpu.bitcast(ws[lt] & hi_mask, F32)
                yacc[lt] = yv if yacc[lt] is None else yacc[lt] + yv
            for r in range(r0, r0 + SUBLANES):
                row_copy(idx2_ref, r, nxt).start(priority=r % 2)
        ys.append(jnp.concatenate([jnp.sum(ya, axis=0, keepdims=True) for ya in yacc], axis=1))
    out = x_ref[...] + gate_ref[...] * jnp.concatenate(ys, axis=0)
    if final_norm:
        out = (out * lax.rsqrt(jnp.mean(out * out, axis=-1, keepdims=True) + EPS)) * fn_ref[...]
    o_ref[...] = out

    @pl.when(i == n - 1)
    def _():
        slot_copy((i + 1) % PEER_SLOTS).wait()
        slot_copy(nxt).wait()


def peer_mix(experts_t, g_t, h, x, gate, table, final_g, *, seq, final_norm, tb=8):
    T, D = h.shape
    nk = experts_t.shape[0]
    B = gate.shape[0]
    nb = seq // tb
    nblk = T // tb
    idx = experts_t.T.reshape(nblk, 1, tb * nk)
    g = g_t.reshape(nk, nblk, tb).transpose(1, 0, 2)
    smem_spec = lambda f: pl.BlockSpec((None, 1, tb * nk), f, memory_space=pltpu.SMEM)
    return pl.pallas_call(
        functools.partial(_peer_mix_kernel, tb=tb, nk=nk, final_norm=final_norm),
        grid=(nblk,),
        in_specs=[
            smem_spec(lambda i: (i, 0, 0)),
            smem_spec(lambda i: (jnp.minimum(i + 1, nblk - 1), 0, 0)),
            smem_spec(lambda i: (jnp.minimum(i + 2, nblk - 1), 0, 0)),
            pl.BlockSpec((tb, D), lambda i: (i, 0)),
            pl.BlockSpec((None, nk, tb), lambda i: (i, 0, 0)),
            pl.BlockSpec((tb, D), lambda i: (i, 0)),
            pl.BlockSpec((None, 1, D), lambda i: (i // nb, 0, 0)),
            pl.BlockSpec((1, D), lambda i: (0, 0)),
            pl.BlockSpec(memory_space=pl.ANY),
        ],
        out_specs=pl.BlockSpec((tb, D), lambda i: (i, 0)),
        out_shape=jax.ShapeDtypeStruct((T, D), F32),
        scratch_shapes=[pltpu.VMEM((PEER_SLOTS, D // LANES, tb * nk, LANES), jnp.uint32),
                        pltpu.SemaphoreType.DMA((PEER_SLOTS,))],
        compiler_params=_cparams(("arbitrary",)),
        name="peer_mix",
    )(idx, idx, idx, h, g, x, gate.reshape(B, 1, D), final_g.reshape(1, D), table)


def _rmsnorm_kernel(x_ref, g_ref, o_ref):
    x = x_ref[...]
    o_ref[...] = (x * lax.rsqrt(jnp.mean(x * x, axis=-1, keepdims=True) + EPS)) * g_ref[...]


def rmsnorm_rows(x, g, tm=512):
    T, D = x.shape
    return pl.pallas_call(
        _rmsnorm_kernel,
        grid=(T // tm,),
        in_specs=[pl.BlockSpec((tm, D), lambda i: (i, 0)), pl.BlockSpec((1, D), lambda i: (0, 0))],
        out_specs=pl.BlockSpec((tm, D), lambda i: (i, 0)),
        out_shape=jax.ShapeDtypeStruct((T, D), F32),
        compiler_params=_cparams(("arbitrary",)),
        name="final_rmsnorm",
    )(x, g.reshape(1, D))


def _split3(m):
    d = m.shape[-1] // 3
    return m[:, :d], m[:, d:2 * d], m[:, 2 * d:]


def sb_mla_layer(x, mod, pos, g_norm, w_in, q_norm, w_uq, kv_norm, w_ukv, w_out, *, batch, seq):
    T, D = x.shape
    shift, scale, gate = _split3(mod)
    sbw = SB_HEADS * SB_DIM
    n_in = w_in.shape[1]
    n_pad = -(-n_in // 512) * 512
    p = norm_matmul(x, g_norm, scale, shift, _pad_cols(w_in, n_pad).astype(BF16), seq=seq, tm=512, tn=512)
    o_a = stick_breaking_attn(p, batch=batch, seq=seq, heads=SB_HEADS, dim=SB_DIM,
                              q_col=0, k_col=SB_HEADS, v_col=2 * SB_HEADS)
    c_q = p[:, 3 * sbw:3 * sbw + MLA_Q_RANK]
    c_kv = p[:, 3 * sbw + MLA_Q_RANK:3 * sbw + MLA_Q_RANK + MLA_KV_RANK]
    k_r = p[:, 3 * sbw + MLA_Q_RANK + MLA_KV_RANK:n_in]
    zq = jnp.zeros((batch, MLA_Q_RANK), F32)
    zkv = jnp.zeros((batch, MLA_KV_RANK), F32)
    dqk = MLA_NOPE + MLA_ROPE
    wq = w_uq.reshape(MLA_Q_RANK, MLA_HEADS, dqk)
    wq = jnp.concatenate([wq[:, :, :MLA_NOPE].reshape(MLA_Q_RANK, -1), wq[:, :, MLA_NOPE:].reshape(MLA_Q_RANK, -1)], 1)
    wkv = w_ukv.reshape(MLA_KV_RANK, MLA_HEADS, MLA_NOPE + MLA_V)
    wkv = jnp.concatenate([wkv[:, :, :MLA_NOPE].reshape(MLA_KV_RANK, -1), wkv[:, :, MLA_NOPE:].reshape(MLA_KV_RANK, -1)], 1)
    qf = norm_matmul(c_q, q_norm, zq, zq, wq.astype(BF16), seq=seq, tm=512, tn=512)
    kvf = norm_matmul(c_kv, kv_norm, zkv, zkv, wkv.astype(BF16), seq=seq, tm=512, tn=512, out_dtype=BF16)
    tables = _rope_tables(pos, MLA_ROPE, MLA_ROPE)
    nope_w = MLA_HEADS * MLA_NOPE
    q_rope = rope_cols(qf, tables, seq=seq, col0=nope_w // (MLA_HEADS * MLA_ROPE), n_heads=MLA_HEADS * MLA_ROPE // LANES,
                       half=MLA_ROPE // 2, tt=512)
    kr_pad = jnp.pad(k_r, ((0, 0), (0, LANES - MLA_ROPE)))
    k_rope = rope_cols(kr_pad, tables, seq=seq, col0=0, n_heads=1, half=MLA_ROPE // 2, tt=512)[:, :MLA_ROPE]
    H = MLA_HEADS
    q_nope = qf[:, :nope_w].astype(BF16).reshape(batch, seq, H, MLA_NOPE)
    q_cat = jnp.concatenate([q_nope, q_rope.reshape(batch, seq, H, MLA_ROPE)], -1)
    tq = min(1024, seq)
    q_cat = q_cat.transpose(0, 2, 1, 3).reshape(batch * H, seq // tq, tq, dqk)
    k_nope = kvf[:, :nope_w].reshape(batch, seq, H, MLA_NOPE)
    k_cat = jnp.concatenate([k_nope, jnp.broadcast_to(k_rope.reshape(batch, seq, 1, MLA_ROPE), (batch, seq, H, MLA_ROPE))], -1)
    k_cat = k_cat.transpose(0, 2, 1, 3).reshape(batch * H, seq, dqk)
    v = kvf[:, nope_w:].reshape(batch, seq, H, MLA_V).transpose(0, 2, 1, 3).reshape(batch * H, seq, MLA_V)
    o_b = flash_attn(q_cat, k_cat, v, qt=tq, tk=256, scale=1.0 / math.sqrt(dqk))
    o_b = o_b.reshape(batch, H, seq, MLA_V).transpose(0, 2, 1, 3).reshape(T, H * MLA_V)
    wo = w_out.astype(BF16)
    return matmul_residual([o_a, o_b], [wo[:sbw], wo[sbw:]], x, gate, seq=seq, tm=512, tn=min(512, D))


def nsa_layer(x, mod, pos, g_norm, w_in, pe_k, pe_v, w1_k, w2_k, w1_v, w2_v, w_out, *, batch, seq):
    T, D = x.shape
    shift, scale, gate = _split3(mod)
    G, R, d = NSA_GROUPS, NSA_HEADS // NSA_GROUPS, NSA_DIM
    H = NSA_HEADS
    kvw = G * d
    n_in = w_in.shape[1]
    n_pad = -(-n_in // 768) * 768
    p = norm_matmul(x, g_norm, scale, shift, _pad_cols(w_in, n_pad).astype(BF16), seq=seq, tm=512, tn=768)
    tables = _rope_tables(pos, ROT_DIM, d)
    half = ROT_DIM // 2
    qt = QBLK
    nQ = seq // qt
    q_st = rope_cols(p, tables, seq=seq, col0=0, n_heads=R, half=half, tt=qt, stacked_groups=G)
    q_st = q_st.reshape(batch, nQ, G, R * qt, d).transpose(0, 2, 1, 3, 4)
    base = H * d // kvw
    kc = rope_cols(p, tables, seq=seq, col0=base + 0, n_heads=G, half=half, tt=512)
    ks = rope_cols(p, tables, seq=seq, col0=base + 2, n_heads=G, half=half, tt=512)
    kw = rope_cols(p, tables, seq=seq, col0=base + 4, n_heads=G, half=half, tt=512)
    off = H * d
    vc = p[:, off + kvw:off + 2 * kvw].astype(BF16)
    vs = p[:, off + 3 * kvw:off + 4 * kvw].astype(BF16)
    vw = p[:, off + 5 * kvw:off + 6 * kvw].astype(BF16)
    gl = jnp.pad(p[:, off + 6 * kvw:n_in], ((0, 0), (0, LANES - 3 * H)))

    def per_group(a):
        return a.reshape(batch, seq, G, d).transpose(0, 2, 1, 3)

    M = seq // CMP_STRIDE
    n_cmp = (seq - CMP_BLK) // CMP_STRIDE + 1
    ncp = -(-M // LANES) * LANES
    def chunks(a):
        c = per_group(a).reshape(batch, G, M, CMP_STRIDE * d)
        return jnp.pad(c, ((0, 0), (0, 0), (0, ncp - M), (0, 0)))
    kc_c = nsa_compress(chunks(kc), pe_k, w1_k, w2_k)
    vc_c = nsa_compress(chunks(vc), pe_v, w1_v, w2_v)
    n_slc = seq // SLC_BLK
    slc_k = min(SLC_TOPK, n_slc)
    c_s = np.arange(ncp) * CMP_STRIDE
    s_s = np.arange(LANES) * SLC_BLK
    ovl = np.clip(np.minimum(c_s[:, None] + CMP_BLK, s_s[None, :] + SLC_BLK) - np.maximum(c_s[:, None], s_s[None, :]), 0, None)
    ovl[n_cmp:, :] = 0
    ovl[:, n_slc:] = 0
    scale_a = 1.0 / math.sqrt(d)
    oc, bias = nsa_cmp_select(q_st, kc_c, vc_c, jnp.asarray(ovl, BF16), qt=qt, n_cmp=n_cmp, n_slc=n_slc,
                              slc_k=slc_k, scale=scale_a)
    q_aug = jnp.concatenate([q_st, jnp.tile(bias, (1, 1, 1, R, 1))], -1).reshape(batch * G, nQ, R * qt, 2 * d)
    onehot = (np.arange(seq)[:, None] // SLC_BLK == np.arange(LANES)[None, :]).astype(np.float32)
    ks_g = per_group(ks)
    k_aug = jnp.concatenate([ks_g, jnp.broadcast_to(jnp.asarray(onehot, BF16), ks_g.shape[:2] + onehot.shape)], -1)
    k_aug = k_aug.reshape(batch * G, seq, 2 * d)
    osl = flash_attn(q_aug, k_aug, per_group(vs).reshape(batch * G, seq, d), qt=qt, tk=256, scale=scale_a)
    q_flat = q_st.reshape(batch * G, nQ, R * qt, d)
    ow = flash_attn(q_flat, per_group(kw).reshape(batch * G, seq, d), per_group(vw).reshape(batch * G, seq, d),
                    qt=qt, tk=128, scale=scale_a, window=WINDOW)

    def unstack(o):
        return o.reshape(batch, G, nQ, R, qt, d).transpose(0, 2, 1, 3, 4, 5).reshape(batch * nQ, G, R, qt, d)

    o = nsa_gate_combine(unstack(oc.reshape(batch * G, nQ, R * qt, d)), unstack(osl), unstack(ow), gl, R=R)
    return matmul_residual([o], [w_out.astype(BF16)], x, gate, seq=seq, tm=512, tn=min(512, D))


def peer_layer(x, mod, g_norm, w_q, k1, k2, u_all, v_all, layer, final_g, *, batch, seq, final_norm):
    shift, scale, gate = _split3(mod)
    q, h = norm_matmul(x, g_norm, scale, shift, w_q.astype(BF16), seq=seq, tm=512, tn=512, emit_h=True)
    experts, g = peer_route(q, k1, k2, heads=PEER_HEADS, topk=PEER_TOPK)
    table = peer_pack(u_all, v_all, layer).reshape(u_all.shape[1], u_all.shape[2] // LANES, LANES)
    return peer_mix(experts, g, h, x, gate, table, final_g, seq=seq, final_norm=final_norm)


def kernel(x, c, positions, norm_mix, ada_mix_w, ada_mix_b, sbmla_w_in, mla_q_norm, mla_w_uq, mla_kv_norm, mla_w_ukv, sbmla_w_out, nsa_w_in, nsa_pe_k, nsa_pe_v, nsa_w1_k, nsa_w2_k, nsa_w1_v, nsa_w2_v, nsa_w_out, norm_ffn, ada_ffn_w, ada_ffn_b, peer_w_q, peer_k1, peer_k2, peer_u, peer_v, final_norm):
    B, S, D = x.shape
    depth = norm_mix.shape[0]
    mod_mix = modulation_all(c, ada_mix_w, ada_mix_b)
    mod_ffn = modulation_all(c, ada_ffn_w, ada_ffn_b)
    xs = x.reshape(B * S, D)
    for layer in range(depth):
        i = layer // 2
        if layer % 2 == 0:
            xs = sb_mla_layer(xs, mod_mix[layer], positions, norm_mix[layer], sbmla_w_in[i], mla_q_norm[i], mla_w_uq[i],
                              mla_kv_norm[i], mla_w_ukv[i], sbmla_w_out[i], batch=B, seq=S)
        else:
            xs = nsa_layer(xs, mod_mix[layer], positions, norm_mix[layer], nsa_w_in[i], nsa_pe_k[i], nsa_pe_v[i],
                           nsa_w1_k[i], nsa_w2_k[i], nsa_w1_v[i], nsa_w2_v[i], nsa_w_out[i], batch=B, seq=S)
        xs = peer_layer(xs, mod_ffn[layer], norm_ffn[layer], peer_w_q[layer], peer_k1[layer], peer_k2[layer],
                        peer_u, peer_v, layer, final_norm, batch=B, seq=S, final_norm=layer == depth - 1)
    return xs.reshape(B, S, D)
```
